```python
import math
import jax
import jax.numpy as jnp
from jax import lax
import numpy as np

D_MODEL = 2048
BATCH = 4
SEQ = 4096
DEPTH = 4

N_MIXERS = 2
N_RET_LAYERS = (DEPTH + 1) // 2
N_GDN_LAYERS = DEPTH // 2
N_MEM = 256
MIX_WIDTH = 2 * D_MODEL
TOK_WIDTH = 3 * MIX_WIDTH // 4
MEM_WIDTH = MIX_WIDTH - TOK_WIDTH
QK_HEAD_DIM = 128
N_QK_HEADS = TOK_WIDTH // 256
QK_WIDTH = N_QK_HEADS * QK_HEAD_DIM
RET_V_HEAD_DIM = TOK_WIDTH // N_QK_HEADS
GDN_V_HEAD_DIM = 128
N_GDN_V_HEADS = TOK_WIDTH // GDN_V_HEAD_DIM
GDN_GQA = N_GDN_V_HEADS // N_QK_HEADS
MEM_HEADS = 4
MEM_HEAD_DIM = MEM_WIDTH // MEM_HEADS
CONV_WIDTH = 4
CHUNK = 64
ROPE_BASE = 10000.0
CONV_CH = 2 * QK_WIDTH + TOK_WIDTH
RET_COLS = CONV_CH + MEM_WIDTH + MIX_WIDTH
GDN_COLS = RET_COLS + 2 * N_GDN_V_HEADS
DEEPNORM_ALPHA = (2.0 * DEPTH) ** 0.25
DEEPNORM_BETA = (8.0 * DEPTH) ** -0.25
LN_EPS = 1e-5
NORM_EPS = 1e-6

kernel_name = 'hybrid_retention_gdn_memory_deepnorm'


def layer_norm(x, g, b):
    xf = x.astype(jnp.float32)
    mu = jnp.mean(xf, -1, keepdims=True)
    var = jnp.mean(jnp.square(xf - mu), -1, keepdims=True)
    return ((xf - mu) * lax.rsqrt(var + LN_EPS) * g.astype(jnp.float32) + b.astype(jnp.float32)).astype(x.dtype)


def rotary(t, positions):
    half = t.shape[-1] // 2
    inv_freq = ROPE_BASE ** (-jnp.arange(half, dtype=jnp.float32) / half)
    ang = positions.astype(jnp.float32)[..., None] * inv_freq
    cos = jnp.cos(ang)[:, :, None, :]
    sin = jnp.sin(ang)[:, :, None, :]
    t1, t2 = t[..., :half], t[..., half:]
    return jnp.concatenate([t1 * cos - t2 * sin, t1 * sin + t2 * cos], -1)


def retention(q, k, v):
    B, S, H, dk = q.shape
    dv = v.shape[-1]
    n = S // CHUNK
    log_gamma = jnp.log1p(-jnp.exp2(-5.0 - jnp.arange(H, dtype=jnp.float32)))
    k = k * dk ** -0.5
    qc = q.reshape(B, n, CHUNK, H, dk)
    kc = k.reshape(B, n, CHUNK, H, dk)
    vc = v.reshape(B, n, CHUNK, H, dv)
    idx = jnp.arange(CHUNK, dtype=jnp.float32)
    rel = idx[:, None] - idx[None, :]
    intra_decay = jnp.where(rel[None] >= 0, jnp.exp(log_gamma[:, None, None] * jnp.maximum(rel, 0.0)[None]), 0.0)
    scores = jnp.einsum('bnihd,bnjhd->bnhij', qc, kc) * intra_decay
    inner = jnp.einsum('bnhij,bnjhv->bnihv', scores, vc)
    q_decay = jnp.exp((idx[:, None] + 1.0) * log_gamma[None, :])
    k_decay = jnp.exp((CHUNK - 1.0 - idx)[:, None] * log_gamma[None, :])
    chunk_decay = jnp.exp(CHUNK * log_gamma)

    def step(state, inp):
        q_i, k_i, v_i = inp
        cross = jnp.einsum('bihd,bhdv->bihv', q_i, state) * q_decay[None, :, :, None]
        state = state * chunk_decay[None, :, None, None] + jnp.einsum('bjhd,bjhv->bhdv', k_i * k_decay[None, :, :, None], v_i)
        return state, cross

    state0 = jnp.zeros((B, H, dk, dv), jnp.float32)
    xs = (jnp.moveaxis(qc, 1, 0), jnp.moveaxis(kc, 1, 0), jnp.moveaxis(vc, 1, 0))
    _, cross = lax.scan(step, state0, xs)
    return (inner + jnp.moveaxis(cross, 0, 1)).reshape(B, S, H, dv)


def _to_chunks(t):
    B, S, H = t.shape[:3]
    t = t.reshape((B, S // CHUNK, CHUNK, H) + t.shape[3:])
    return jnp.moveaxis(t, 3, 1)


def gated_delta_rule(q, k, v, g, beta):
    B, S, H, dk = q.shape
    dv = v.shape[-1]
    qc = _to_chunks(q * dk ** -0.5)
    kc = _to_chunks(k)
    vc = _to_chunks(v)
    bc = _to_chunks(beta)
    g_cum = jnp.cumsum(_to_chunks(g), -1)
    causal = jnp.tril(jnp.ones((CHUNK, CHUNK), bool))
    strict = jnp.tril(jnp.ones((CHUNK, CHUNK), bool), -1)
    diff = g_cum[..., :, None] - g_cum[..., None, :]
    decay = jnp.where(causal, jnp.exp(jnp.where(causal, diff, 0.0)), 0.0)
    k_beta = kc * bc[..., None]
    v_beta = vc * bc[..., None]
    lower = jnp.where(strict, jnp.einsum('bhnid,bhnjd->bhnij', k_beta, kc) * decay, 0.0)
    a_mat = lower + jnp.eye(CHUNK, dtype=jnp.float32)
    u = lax.linalg.triangular_solve(a_mat, v_beta, left_side=True, lower=True, unit_diagonal=True)
    w = lax.linalg.triangular_solve(a_mat, k_beta * jnp.exp(g_cum)[..., None], left_side=True, lower=True, unit_diagonal=True)
    qk = jnp.where(causal, jnp.einsum('bhnid,bhnjd->bhnij', qc, kc) * decay, 0.0)
    q_g = qc * jnp.exp(g_cum)[..., None]
    k_tail = kc * jnp.exp(g_cum[..., -1:] - g_cum)[..., None]
    g_last = jnp.exp(g_cum[..., -1])

    def step(state, inp):
        u_i, w_i, qk_i, q_i, k_i, gl_i = inp
        v_new = u_i - jnp.einsum('bhcd,bhdv->bhcv', w_i, state)
        out = jnp.einsum('bhcd,bhdv->bhcv', q_i, state) + jnp.einsum('bhij,bhjv->bhiv', qk_i, v_new)
        state = state * gl_i[..., None, None] + jnp.einsum('bhcd,bhcv->bhdv', k_i, v_new)
        return state, out

    state0 = jnp.zeros((B, H, dk, dv), jnp.float32)
    xs = tuple(jnp.moveaxis(t, 2, 0) for t in (u, w, qk, q_g, k_tail, g_last))
    _, out = lax.scan(step, state0, xs)
    return out.transpose(1, 0, 3, 2, 4).reshape(B, S, H, dv)


def causal_conv_silu(t, w):
    S = t.shape[1]
    tp = jnp.pad(t, ((0, 0), (CONV_WIDTH - 1, 0), (0, 0)))
    y = tp[:, 0:S, :] * w[0]
    for j in range(1, CONV_WIDTH):
        y = y + tp[:, j:j + S, :] * w[j]
    return jax.nn.silu(y)


def retention_branch(x, positions, w_in, norm_g):
    B, S, _ = x.shape
    h = x @ w_in
    q = h[..., :QK_WIDTH].reshape(B, S, N_QK_HEADS, QK_HEAD_DIM).astype(jnp.float32)
    k = h[..., QK_WIDTH:2 * QK_WIDTH].reshape(B, S, N_QK_HEADS, QK_HEAD_DIM).astype(jnp.float32)
    v = h[..., 2 * QK_WIDTH:CONV_CH].reshape(B, S, N_QK_HEADS, RET_V_HEAD_DIM).astype(jnp.float32)
    mq = h[..., CONV_CH:CONV_CH + MEM_WIDTH]
    z = h[..., CONV_CH + MEM_WIDTH:CONV_CH + MEM_WIDTH + MIX_WIDTH]
    o = retention(rotary(q, positions), rotary(k, positions), v)
    mu = jnp.mean(o, -1, keepdims=True)
    var = jnp.mean(jnp.square(o - mu), -1, keepdims=True)
    o = ((o - mu) * lax.rsqrt(var + NORM_EPS)).reshape(B, S, TOK_WIDTH) * norm_g.astype(jnp.float32)
    return o.astype(x.dtype), mq, z


def gdn_branch(x, w_in, conv_w, a_log, dt_bias, norm_g):
    B, S, _ = x.shape
    h = x @ w_in
    qkv = causal_conv_silu(h[..., :CONV_CH], conv_w).astype(jnp.float32)
    mq = h[..., CONV_CH:CONV_CH + MEM_WIDTH]
    z = h[..., CONV_CH + MEM_WIDTH:CONV_CH + MEM_WIDTH + MIX_WIDTH]
    a = h[..., RET_COLS:RET_COLS + N_GDN_V_HEADS].astype(jnp.float32)
    b = h[..., RET_COLS + N_GDN_V_HEADS:].astype(jnp.float32)
    q = qkv[..., :QK_WIDTH].reshape(B, S, N_QK_HEADS, QK_HEAD_DIM)
    k = qkv[..., QK_WIDTH:2 * QK_WIDTH].reshape(B, S, N_QK_HEADS, QK_HEAD_DIM)
    v = qkv[..., 2 * QK_WIDTH:].reshape(B, S, N_GDN_V_HEADS, GDN_V_HEAD_DIM)
    q = q * lax.rsqrt(jnp.sum(jnp.square(q), -1, keepdims=True) + NORM_EPS)
    k = k * lax.rsqrt(jnp.sum(jnp.square(k), -1, keepdims=True) + NORM_EPS)
    q = jnp.repeat(q, GDN_GQA, axis=2)
    k = jnp.repeat(k, GDN_GQA, axis=2)
    beta = jax.nn.sigmoid(b)
    g = -jnp.exp(a_log.astype(jnp.float32)) * jax.nn.softplus(a + dt_bias.astype(jnp.float32))
    o = gated_delta_rule(q, k, v, g, beta)
    o = o * lax.rsqrt(jnp.mean(jnp.square(o), -1, keepdims=True) + NORM_EPS) * norm_g.astype(jnp.float32)
    return o.reshape(B, S, TOK_WIDTH).astype(x.dtype), mq, z


def memory_attention(mq, mem, w_kv):
    B, S, _ = mq.shape
    kv = mem @ w_kv
    mk = kv[..., :MEM_WIDTH].reshape(B, N_MEM, MEM_HEADS, MEM_HEAD_DIM)
    mv = kv[..., MEM_WIDTH:].reshape(B, N_MEM, MEM_HEADS, MEM_HEAD_DIM)
    q = mq.reshape(B, S, MEM_HEADS, MEM_HEAD_DIM)
    s = jnp.einsum('bshd,bmhd->bhsm', q, mk).astype(jnp.float32) * MEM_HEAD_DIM ** -0.5
    p = jax.nn.softmax(s, -1).astype(mv.dtype)
    return jnp.einsum('bhsm,bmhd->bshd', p, mv).reshape(B, S, MEM_WIDTH)


def setup_inputs(seed: int = 0) -> dict:
    key = jax.random.key(seed)
    ks = jax.random.split(key, 14)
    f32 = jnp.float32
    x = jax.random.normal(ks[0], (BATCH, SEQ, D_MODEL), f32)
    mem = jax.random.normal(ks[1], (BATCH, N_MEM, D_MODEL), f32)
    positions = jnp.arange(SEQ, dtype=jnp.int32)[None, :] + jax.random.randint(ks[2], (BATCH, 1), 0, 1024, dtype=jnp.int32)
    w_in_ret = jax.random.normal(ks[3], (N_RET_LAYERS, D_MODEL, RET_COLS), f32) * D_MODEL ** -0.5
    ret_norm_g = 1.0 + 0.02 * jax.random.normal(ks[4], (N_RET_LAYERS, TOK_WIDTH), f32)
    w_in_gdn = jax.random.normal(ks[5], (N_GDN_LAYERS, D_MODEL, GDN_COLS), f32) * D_MODEL ** -0.5
    conv_w = jax.random.normal(ks[6], (N_GDN_LAYERS, CONV_WIDTH, CONV_CH), f32) * CONV_WIDTH ** -0.5
    a_log = jnp.log(jax.random.uniform(ks[7], (N_GDN_LAYERS, N_GDN_V_HEADS), f32, 1.0, 16.0))
    dt = jnp.exp(jax.random.uniform(ks[8], (N_GDN_LAYERS, N_GDN_V_HEADS), f32, math.log(1e-3), math.log(1e-1)))
    dt_bias = dt + jnp.log(-jnp.expm1(-dt))
    gdn_norm_g = 1.0 + 0.02 * jax.random.normal(ks[9], (N_GDN_LAYERS, GDN_V_HEAD_DIM), f32)
    w_mem_kv = jax.random.normal(ks[10], (DEPTH, D_MODEL, 2 * MEM_WIDTH), f32) * D_MODEL ** -0.5
    w_out = jax.random.normal(ks[11], (DEPTH, MIX_WIDTH, D_MODEL), f32) * (MIX_WIDTH ** -0.5 * DEEPNORM_BETA)
    ln_g = 1.0 + 0.02 * jax.random.normal(ks[12], (DEPTH, D_MODEL), f32)
    ln_b = 0.02 * jax.random.normal(ks[13], (DEPTH, D_MODEL), f32)
    return {'x': x, 'mem': mem, 'positions': positions, 'w_in_ret': w_in_ret, 'ret_norm_g': ret_norm_g,
            'w_in_gdn': w_in_gdn, 'conv_w': conv_w, 'a_log': a_log, 'dt_bias': dt_bias, 'gdn_norm_g': gdn_norm_g,
            'w_mem_kv': w_mem_kv, 'w_out': w_out, 'ln_g': ln_g, 'ln_b': ln_b}


def reference(x, mem, positions, w_in_ret, ret_norm_g, w_in_gdn, conv_w, a_log, dt_bias, gdn_norm_g,
              w_mem_kv, w_out, ln_g, ln_b):
    for i in range(DEPTH):
        j = i // N_MIXERS
        if i % N_MIXERS == 0:
            tok, mq, z = retention_branch(x, positions, w_in_ret[j], ret_norm_g[j])
        else:
            tok, mq, z = gdn_branch(x, w_in_gdn[j], conv_w[j], a_log[j], dt_bias[j], gdn_norm_g[j])
        mem_out = memory_attention(mq, mem, w_mem_kv[i])
        branch = jnp.concatenate([tok, mem_out.astype(tok.dtype)], -1) * jax.nn.silu(z)
        y = branch @ w_out[i]
        x = layer_norm(DEEPNORM_ALPHA * x + y, ln_g[i], ln_b[i])
    return x
```

```python
import functools
import math

import numpy as np
import jax
import jax.numpy as jnp
from jax import lax
from jax.experimental import pallas as pl
from jax.experimental.pallas import tpu as pltpu

QK_HEAD_DIM = 128
N_QK_HEADS = 12
QK_WIDTH = N_QK_HEADS * QK_HEAD_DIM
TOK_WIDTH = 3072
RET_V_HEAD_DIM = 256
GDN_V_HEAD_DIM = 128
N_GDN_V_HEADS = 24
MEM_HEADS = 4
MEM_HEAD_DIM = 256
MEM_WIDTH = MEM_HEADS * MEM_HEAD_DIM
MIX_WIDTH = TOK_WIDTH + MEM_WIDTH
CONV_WIDTH = 4
CONV_CH = 2 * QK_WIDTH + TOK_WIDTH
RET_COLS = CONV_CH + MEM_WIDTH + MIX_WIDTH
ROPE_BASE = 10000.0
DEPTH = 4
DEEPNORM_ALPHA = (2.0 * DEPTH) ** 0.25
LN_EPS = 1e-5
NORM_EPS = 1e-6

V7X_VMEM_BYTES = 64 * 1024 * 1024
VMEM_LIMIT = V7X_VMEM_BYTES - 8 * 1024 * 1024
LANES = 128

RET_CHUNK = 256
GDN_CHUNK = 128
GDN_ROWS = 512
CONV_HALO = 8

_F32 = jnp.float32
_BF16 = jnp.bfloat16


def _cparams(sem):
    return pltpu.CompilerParams(dimension_semantics=sem, vmem_limit_bytes=VMEM_LIMIT)


def _dot(a, b):
    return jnp.dot(a, b, preferred_element_type=_F32)


def _dot_nt(a, b):
    return lax.dot_general(a, b, (((1,), (1,)), ((), ())), preferred_element_type=_F32)


def _dot_tn(a, b):
    return lax.dot_general(a, b, (((0,), (0,)), ((), ())), preferred_element_type=_F32)


def _mm_kernel(x_ref, w_ref, o_ref):
    o_ref[...] = _dot(x_ref[...], w_ref[...]).astype(o_ref.dtype)


def _matmul(x, w, out_dtype, tm=1024, tn=1024):
    m, k = x.shape
    n = w.shape[1]
    tm = min(tm, m)
    tn = min(tn, n)
    assert m % tm == 0 and n % tn == 0
    return pl.pallas_call(
        _mm_kernel,
        grid=(m // tm, n // tn),
        in_specs=[pl.BlockSpec((tm, k), lambda i, j: (i, 0)),
                  pl.BlockSpec((k, tn), lambda i, j: (0, j))],
        out_specs=pl.BlockSpec((tm, tn), lambda i, j: (i, j)),
        out_shape=jax.ShapeDtypeStruct((m, n), out_dtype),
        compiler_params=_cparams(("parallel", "arbitrary")),
        name="proj_matmul",
    )(x, w)


def _rope_kernel(pos_ref, freq_ref, cos_ref, sin_ref):
    ang = pos_ref[0].astype(_F32) * freq_ref[...]
    lane = lax.broadcasted_iota(jnp.int32, ang.shape, 1)
    s = jnp.sin(ang)
    cos_ref[0] = jnp.cos(ang)
    sin_ref[0] = jnp.where(lane < QK_HEAD_DIM // 2, -s, s)


def _rope_tables(positions, ts=512):
    b, s = positions.shape
    half = QK_HEAD_DIM // 2
    inv_freq = (ROPE_BASE ** (-np.arange(half, dtype=np.float32) / np.float32(half))).astype(np.float32)
    freq = jnp.asarray(np.concatenate([inv_freq, inv_freq])[None, :])
    ts = min(ts, s)
    out = jax.ShapeDtypeStruct((b, s, QK_HEAD_DIM), _F32)
    return pl.pallas_call(
        _rope_kernel,
        grid=(b, s // ts),
        in_specs=[pl.BlockSpec((1, ts, 1), lambda i, j: (i, j, 0)),
                  pl.BlockSpec((1, QK_HEAD_DIM), lambda i, j: (0, 0))],
        out_specs=[pl.BlockSpec((1, ts, QK_HEAD_DIM), lambda i, j: (i, j, 0))] * 2,
        out_shape=[out, out],
        compiler_params=_cparams(("parallel", "parallel")),
        name="rope_tables",
    )(positions.reshape(b, s, 1), freq)


def _retention_kernel(lg_ref, q_ref, k_ref, v_ref, cos_ref, sin_ref, g_ref, o_ref, state_ref):
    c = q_ref.shape[1]
    lg = lg_ref[pl.program_id(1)]

    @pl.when(pl.program_id(2) == 0)
    def _():
        state_ref[...] = jnp.zeros_like(state_ref)

    cos = cos_ref[0]
    sin = sin_ref[0]
    q = q_ref[0].astype(_F32)
    k = k_ref[0].astype(_F32)
    half = QK_HEAD_DIM // 2
    qr = q * cos + pltpu.roll(q, half, 1) * sin
    kr = (k * cos + pltpu.roll(k, half, 1) * sin) * (QK_HEAD_DIM ** -0.5)
    vb = v_ref[0].astype(_BF16)

    ii = lax.broadcasted_iota(jnp.int32, (c, c), 0)
    jj = lax.broadcasted_iota(jnp.int32, (c, c), 1)
    rel = (ii - jj).astype(_F32)
    decay = jnp.where(rel >= 0, jnp.exp(lg * jnp.maximum(rel, 0.0)), 0.0)
    scores = _dot_nt(qr.astype(_BF16), kr.astype(_BF16)) * decay
    inner = _dot(scores.astype(_BF16), vb)

    idx = lax.broadcasted_iota(jnp.int32, (c, QK_HEAD_DIM), 0).astype(_F32)
    q_decay = jnp.exp((idx + 1.0) * lg)
    k_decay = jnp.exp((c - 1.0 - idx) * lg)
    state = state_ref[...]
    cross = _dot((qr * q_decay).astype(_BF16), state.astype(_BF16))
    state_ref[...] = state * jnp.exp(c * lg) + _dot_tn((kr * k_decay).astype(_BF16), vb)

    o = inner + cross
    mu = jnp.mean(o, -1, keepdims=True)
    d = o - mu
    var = jnp.mean(d * d, -1, keepdims=True)
    o_ref[0] = (d * lax.rsqrt(var + NORM_EPS) * g_ref[...]).astype(o_ref.dtype)


def _retention(qkv, cos, sin, norm_g, c=RET_CHUNK):
    b, s, _ = qkv.shape
    c = min(c, s)
    log_gamma = jnp.asarray(np.log1p(-np.exp2(-5.0 - np.arange(N_QK_HEADS, dtype=np.float64))).astype(np.float32))
    v_blk0 = 2 * QK_WIDTH // RET_V_HEAD_DIM
    grid_spec = pltpu.PrefetchScalarGridSpec(
        num_scalar_prefetch=1,
        grid=(b, N_QK_HEADS, s // c),
        in_specs=[
            pl.BlockSpec((1, c, QK_HEAD_DIM), lambda i, h, t, lg: (i, t, h)),
            pl.BlockSpec((1, c, QK_HEAD_DIM), lambda i, h, t, lg: (i, t, N_QK_HEADS + h)),
            pl.BlockSpec((1, c, RET_V_HEAD_DIM), lambda i, h, t, lg: (i, t, v_blk0 + h)),
            pl.BlockSpec((1, c, QK_HEAD_DIM), lambda i, h, t, lg: (i, t, 0)),
            pl.BlockSpec((1, c, QK_HEAD_DIM), lambda i, h, t, lg: (i, t, 0)),
            pl.BlockSpec((1, RET_V_HEAD_DIM), lambda i, h, t, lg: (0, h)),
        ],
        out_specs=pl.BlockSpec((1, c, RET_V_HEAD_DIM), lambda i, h, t, lg: (i, t, h)),
        scratch_shapes=[pltpu.VMEM((QK_HEAD_DIM, RET_V_HEAD_DIM), _F32)],
    )
    return pl.pallas_call(
        _retention_kernel,
        grid_spec=grid_spec,
        out_shape=jax.ShapeDtypeStruct((b, s, TOK_WIDTH), _BF16),
        compiler_params=_cparams(("parallel", "parallel", "arbitrary")),
        name="retention",
    )(log_gamma, qkv, qkv, qkv, cos, sin, norm_g.reshape(1, TOK_WIDTH))


def _gdn_gate_kernel(x_ref, w_ref, alog_ref, dtb_ref, gc_ref, beta_ref):
    ab = _dot_nt(w_ref[...], x_ref[0])
    a = ab[:N_GDN_V_HEADS]
    bb = ab[N_GDN_V_HEADS:]
    g = -jnp.exp(alog_ref[...]) * jax.nn.softplus(a + dtb_ref[...])
    lane = lax.broadcasted_iota(jnp.int32, g.shape, 1) % GDN_CHUNK
    shift = 1
    while shift < GDN_CHUNK:
        g = g + jnp.where(lane >= shift, pltpu.roll(g, shift, 1), 0.0)
        shift *= 2
    gc_ref[0] = g
    beta_ref[0] = jax.nn.sigmoid(bb)


def _gdn_gates(xb, w_ab_t, a_log, dt_bias, ts=512):
    b, s, d = xb.shape
    ts = min(ts, s)
    out = jax.ShapeDtypeStruct((b, N_GDN_V_HEADS, s), _F32)
    return pl.pallas_call(
        _gdn_gate_kernel,
        grid=(b, s // ts),
        in_specs=[pl.BlockSpec((1, ts, d), lambda i, j: (i, j, 0)),
                  pl.BlockSpec((2 * N_GDN_V_HEADS, d), lambda i, j: (0, 0)),
                  pl.BlockSpec((N_GDN_V_HEADS, 1), lambda i, j: (0, 0)),
                  pl.BlockSpec((N_GDN_V_HEADS, 1), lambda i, j: (0, 0))],
        out_specs=[pl.BlockSpec((1, N_GDN_V_HEADS, ts), lambda i, j: (i, 0, j))] * 2,
        out_shape=[out, out],
        compiler_params=_cparams(("parallel", "parallel")),
        name="gdn_gates",
    )(xb, w_ab_t, a_log.reshape(-1, 1).astype(_F32), dt_bias.reshape(-1, 1).astype(_F32))


def _dot_hi(a, b):
    return jnp.dot(a, b, preferred_element_type=_F32, precision=lax.Precision.HIGHEST)


def _unit_lower_inverse(low, ii, jj):
    n = low.shape[0]
    xor = ii ^ jj
    eye = (ii == jj).astype(_F32)
    d1 = jnp.where(xor < 8, low, 0.0)
    d2 = _dot_hi(d1, d1)
    d4 = _dot_hi(d2, d2)
    t = eye - d1
    t = t + _dot_hi(t, d2)
    t = t + _dot_hi(t, d4)
    size = 8
    while size < n:
        off = jnp.where((xor >= size) & (xor < 2 * size), low, 0.0)
        t = t - _dot_hi(_dot_hi(t, off), t)
        size *= 2
    return t


def _causal_conv_silu(buf_ref, x, w, first):
    rows = x.shape[0]

    @pl.when(first)
    def _():
        buf_ref[0:CONV_HALO, :] = jnp.zeros((CONV_HALO, x.shape[1]), _F32)

    buf_ref[CONV_HALO:CONV_HALO + rows, :] = x
    y = x * w[CONV_WIDTH - 1:CONV_WIDTH, :]
    for lag in range(1, CONV_WIDTH):
        y = y + buf_ref[CONV_HALO - lag:CONV_HALO - lag + rows, :] * w[CONV_WIDTH - 1 - lag:CONV_WIDTH - lag, :]
    buf_ref[0:CONV_HALO, :] = buf_ref[rows:rows + CONV_HALO, :]
    return y * jax.nn.sigmoid(y)


def _gdn_kernel(q_ref, k_ref, v_ref, wq_ref, wk_ref, wv_ref, gc_ref, beta_ref, ng_ref, o_ref,
                qbuf, kbuf, vbuf, state_ref):
    rows = q_ref.shape[1]
    n = GDN_CHUNK
    first = pl.program_id(2) == 0

    @pl.when(first)
    def _():
        state_ref[...] = jnp.zeros_like(state_ref)

    q_all = _causal_conv_silu(qbuf, q_ref[0].astype(_F32), wq_ref[...], first)
    k_all = _causal_conv_silu(kbuf, k_ref[0].astype(_F32), wk_ref[...], first)
    v_all = _causal_conv_silu(vbuf, v_ref[0].astype(_F32), wv_ref[...], first)

    ii = lax.broadcasted_iota(jnp.int32, (n, n), 0)
    jj = lax.broadcasted_iota(jnp.int32, (n, n), 1)
    causal = ii >= jj
    strict = ii > jj
    ng = ng_ref[...]

    for c in range(rows // n):
        r0 = c * n
        qc = q_all[r0:r0 + n]
        kc = k_all[r0:r0 + n]
        qn = qc * (lax.rsqrt(jnp.sum(qc * qc, -1, keepdims=True) + NORM_EPS) * (QK_HEAD_DIM ** -0.5))
        kn = kc * lax.rsqrt(jnp.sum(kc * kc, -1, keepdims=True) + NORM_EPS)
        kb = kn.astype(_BF16)
        kk = _dot_nt(kb, kb)
        qk = _dot_nt(qn.astype(_BF16), kb)
        for e in range(2):
            g_row = jnp.broadcast_to(gc_ref[0, 0, e:e + 1, r0:r0 + n], (n, n))
            g_col = g_row.T
            b_col = jnp.broadcast_to(beta_ref[0, 0, e:e + 1, r0:r0 + n], (n, n)).T
            decay = jnp.where(causal, jnp.exp(jnp.where(causal, g_col - g_row, 0.0)), 0.0)
            t_inv = _unit_lower_inverse(jnp.where(strict, b_col * kk * decay, 0.0), ii, jj)
            p = qk * decay
            e_g = jnp.exp(g_col)
            g_last = g_col[n - 1:n, :]
            k_tail = kn * jnp.exp(g_last - g_col)
            q_g = qn * e_g

            state = state_ref[e]
            both = _dot(jnp.concatenate([q_g, kn], 0).astype(_BF16), state.astype(_BF16))
            v_e = v_all[r0:r0 + n, e * GDN_V_HEAD_DIM:(e + 1) * GDN_V_HEAD_DIM]
            rhs = b_col * (v_e - e_g * both[n:])
            v_new = _dot(t_inv.astype(_BF16), rhs.astype(_BF16)).astype(_BF16)
            out = both[:n] + _dot(p.astype(_BF16), v_new)
            state_ref[e] = state * jnp.exp(g_last) + _dot_tn(k_tail.astype(_BF16), v_new)

            out = out * lax.rsqrt(jnp.mean(out * out, -1, keepdims=True) + NORM_EPS) * ng
            o_ref[0, r0:r0 + n, e * GDN_V_HEAD_DIM:(e + 1) * GDN_V_HEAD_DIM] = out.astype(o_ref.dtype)


def _gdn(qkv, conv_w, gc, beta, norm_g, rows=GDN_ROWS):
    b, s, _ = qkv.shape
    rows = min(rows, s)
    v_blk0 = 2 * QK_WIDTH // (2 * GDN_V_HEAD_DIM)
    gc = gc.reshape(b, N_QK_HEADS, 2, s)
    beta = beta.reshape(b, N_QK_HEADS, 2, s)
    pair = 2 * GDN_V_HEAD_DIM
    return pl.pallas_call(
        _gdn_kernel,
        grid=(b, N_QK_HEADS, s // rows),
        in_specs=[
            pl.BlockSpec((1, rows, QK_HEAD_DIM), lambda i, h, t: (i, t, h)),
            pl.BlockSpec((1, rows, QK_HEAD_DIM), lambda i, h, t: (i, t, N_QK_HEADS + h)),
            pl.BlockSpec((1, rows, pair), lambda i, h, t: (i, t, v_blk0 + h)),
            pl.BlockSpec((CONV_WIDTH, QK_HEAD_DIM), lambda i, h, t: (0, h)),
            pl.BlockSpec((CONV_WIDTH, QK_HEAD_DIM), lambda i, h, t: (0, N_QK_HEADS + h)),
            pl.BlockSpec((CONV_WIDTH, pair), lambda i, h, t: (0, v_blk0 + h)),
            pl.BlockSpec((1, 1, 2, rows), lambda i, h, t: (i, h, 0, t)),
            pl.BlockSpec((1, 1, 2, rows), lambda i, h, t: (i, h, 0, t)),
            pl.BlockSpec((1, GDN_V_HEAD_DIM), lambda i, h, t: (0, 0)),
        ],
        out_specs=pl.BlockSpec((1, rows, pair), lambda i, h, t: (i, t, h)),
        out_shape=jax.ShapeDtypeStruct((b, s, TOK_WIDTH), _BF16),
        scratch_shapes=[pltpu.VMEM((rows + CONV_HALO, QK_HEAD_DIM), _F32),
                        pltpu.VMEM((rows + CONV_HALO, QK_HEAD_DIM), _F32),
                        pltpu.VMEM((rows + CONV_HALO, pair), _F32),
                        pltpu.VMEM((2, QK_HEAD_DIM, GDN_V_HEAD_DIM), _F32)],
        compiler_params=_cparams(("parallel", "parallel", "arbitrary")),
        name="gated_delta",
    )(qkv, qkv, qkv, conv_w, conv_w, conv_w, gc, beta, norm_g.reshape(1, GDN_V_HEAD_DIM))


def _gate_kernel(tok_ref, mqz_ref, kv_ref, o_ref):
    z_tok = mqz_ref[0, :, MEM_WIDTH:MEM_WIDTH + TOK_WIDTH].astype(_F32)
    o_ref[0, :, :TOK_WIDTH] = (tok_ref[0].astype(_F32) * (z_tok * jax.nn.sigmoid(z_tok))).astype(o_ref.dtype)
    for m in range(MEM_HEADS):
        lo = m * MEM_HEAD_DIM
        q = mqz_ref[0, :, lo:lo + MEM_HEAD_DIM]
        mk = kv_ref[0, :, lo:lo + MEM_HEAD_DIM]
        mv = kv_ref[0, :, MEM_WIDTH + lo:MEM_WIDTH + lo + MEM_HEAD_DIM]
        sc = _dot_nt(q, mk) * (MEM_HEAD_DIM ** -0.5)
        sc = sc - jnp.max(sc, -1, keepdims=True)
        p = jnp.exp(sc)
        p = p / jnp.sum(p, -1, keepdims=True)
        att = _dot(p.astype(_BF16), mv)
        z = mqz_ref[0, :, MEM_WIDTH + TOK_WIDTH + lo:MEM_WIDTH + TOK_WIDTH + lo + MEM_HEAD_DIM].astype(_F32)
        o_ref[0, :, TOK_WIDTH + lo:TOK_WIDTH + lo + MEM_HEAD_DIM] = (att * (z * jax.nn.sigmoid(z))).astype(o_ref.dtype)


def _gate(tok, mqz, kv, ts=512):
    b, s, _ = tok.shape
    ts = min(ts, s)
    n_mem = kv.shape[1]
    return pl.pallas_call(
        _gate_kernel,
        grid=(b, s // ts),
        in_specs=[pl.BlockSpec((1, ts, TOK_WIDTH), lambda i, j: (i, j, 0)),
                  pl.BlockSpec((1, ts, MEM_WIDTH + MIX_WIDTH), lambda i, j: (i, j, 0)),
                  pl.BlockSpec((1, n_mem, 2 * MEM_WIDTH), lambda i, j: (i, 0, 0))],
        out_specs=pl.BlockSpec((1, ts, MIX_WIDTH), lambda i, j: (i, j, 0)),
        out_shape=jax.ShapeDtypeStruct((b, s, MIX_WIDTH), _BF16),
        compiler_params=_cparams(("parallel", "parallel")),
        name="mem_attn_gate",
    )(tok, mqz, kv)


def _out_ln_kernel(br_ref, w_ref, x_ref, g_ref, b_ref, o_ref, ob_ref):
    y = _dot(br_ref[...], w_ref[...])
    r = DEEPNORM_ALPHA * x_ref[...] + y
    mu = jnp.mean(r, -1, keepdims=True)
    d = r - mu
    var = jnp.mean(d * d, -1, keepdims=True)
    out = d * lax.rsqrt(var + LN_EPS) * g_ref[...] + b_ref[...]
    o_ref[...] = out
    ob_ref[...] = out.astype(_BF16)


def _out_ln(branch, w_out, x, ln_g, ln_b, tm=256):
    m, k = branch.shape
    d = w_out.shape[1]
    tm = min(tm, m)
    return pl.pallas_call(
        _out_ln_kernel,
        grid=(m // tm,),
        in_specs=[pl.BlockSpec((tm, k), lambda i: (i, 0)),
                  pl.BlockSpec((k, d), lambda i: (0, 0)),
                  pl.BlockSpec((tm, d), lambda i: (i, 0)),
                  pl.BlockSpec((1, d), lambda i: (0, 0)),
                  pl.BlockSpec((1, d), lambda i: (0, 0))],
        out_specs=[pl.BlockSpec((tm, d), lambda i: (i, 0))] * 2,
        out_shape=[jax.ShapeDtypeStruct((m, d), _F32), jax.ShapeDtypeStruct((m, d), _BF16)],
        compiler_params=_cparams(("parallel",)),
        name="out_proj_layernorm",
    )(branch, w_out, x, ln_g.reshape(1, d), ln_b.reshape(1, d))


def kernel(x, mem, positions, w_in_ret, ret_norm_g, w_in_gdn, conv_w, a_log, dt_bias, gdn_norm_g,
           w_mem_kv, w_out, ln_g, ln_b):
    b, s, d = x.shape
    m = b * s
    n_mem = mem.shape[1]
    cos, sin = _rope_tables(positions)
    mem_b = mem.reshape(b * n_mem, d).astype(_BF16)
    xf = x.reshape(m, d)
    xb = xf.astype(_BF16)
    for i in range(DEPTH):
        j = i // 2
        w_in = w_in_ret[j] if i % 2 == 0 else w_in_gdn[j]
        qkv = _matmul(xb, w_in[:, :CONV_CH].astype(_BF16), _F32).reshape(b, s, CONV_CH)
        mqz = _matmul(xb, w_in[:, CONV_CH:RET_COLS].astype(_BF16), _BF16).reshape(b, s, MEM_WIDTH + MIX_WIDTH)
        if i % 2 == 0:
            tok = _retention(qkv, cos, sin, ret_norm_g[j])
        else:
            gc, beta = _gdn_gates(xb.reshape(b, s, d), w_in[:, RET_COLS:].T.astype(_BF16), a_log[j], dt_bias[j])
            tok = _gdn(qkv, conv_w[j], gc, beta, gdn_norm_g[j])
        kv = _matmul(mem_b, w_mem_kv[i].astype(_BF16), _BF16).reshape(b, n_mem, 2 * MEM_WIDTH)
        branch = _gate(tok, mqz, kv).reshape(m, MIX_WIDTH)
        xf, xb = _out_ln(branch, w_out[i].astype(_BF16), xf, ln_g[i], ln_b[i])
    return xf.reshape(b, s, d)
```

```python
import functools

import numpy as np
import jax
import jax.numpy as jnp
from jax import lax
from jax.experimental import pallas as pl
from jax.experimental.pallas import tpu as pltpu

QK_HEAD_DIM = 128
N_QK_HEADS = 12
QK_WIDTH = N_QK_HEADS * QK_HEAD_DIM
TOK_WIDTH = 3072
RET_V_HEAD_DIM = 256
GDN_V_HEAD_DIM = 128
N_GDN_V_HEADS = 24
MEM_HEADS = 4
MEM_HEAD_DIM = 256
MEM_WIDTH = MEM_HEADS * MEM_HEAD_DIM
MIX_WIDTH = TOK_WIDTH + MEM_WIDTH
CONV_WIDTH = 4
CONV_CH = 2 * QK_WIDTH + TOK_WIDTH
RET_COLS = CONV_CH + MEM_WIDTH + MIX_WIDTH
ROPE_BASE = 10000.0
DEPTH = 4
DEEPNORM_ALPHA = (2.0 * DEPTH) ** 0.25
LN_EPS = 1e-5
NORM_EPS = 1e-6

V7X_VMEM_BYTES = 64 * 1024 * 1024
VMEM_LIMIT = V7X_VMEM_BYTES - 8 * 1024 * 1024

PROJ_TILE = 1024
RET_CHUNK = 512
GDN_CHUNK = 128
GDN_ROWS = 512
CONV_HALO = 8
OUT_ROWS = 256

_F32 = jnp.float32
_BF16 = jnp.bfloat16


def _cparams(sem):
    return pltpu.CompilerParams(dimension_semantics=sem, vmem_limit_bytes=VMEM_LIMIT)


def _dot(a, b):
    return jnp.dot(a, b, preferred_element_type=_F32)


def _bdot(a, b):
    return _dot(a.astype(_BF16), b.astype(_BF16))


def _dot_nt(a, b):
    return lax.dot_general(a, b, (((1,), (1,)), ((), ())), preferred_element_type=_F32)


def _dot_tn(a, b):
    return lax.dot_general(a, b, (((0,), (0,)), ((), ())), preferred_element_type=_F32)


def _silu(z):
    return z * jax.nn.sigmoid(z)


def _proj_kernel(x_ref, w_ref, o_ref, wb_ref):
    @pl.when(pl.program_id(1) == 0)
    def _():
        wb_ref[...] = w_ref[0].astype(_BF16)

    o_ref[...] = _dot(x_ref[...], wb_ref[...]).astype(o_ref.dtype)


def _project(x, w_stack, layer, n_cols, tile=PROJ_TILE):
    m, k = x.shape
    tm = min(tile, m)
    assert m % tm == 0 and n_cols % tile == 0
    return pl.pallas_call(
        _proj_kernel,
        grid=(n_cols // tile, m // tm),
        in_specs=[pl.BlockSpec((tm, k), lambda j, i: (i, 0)),
                  pl.BlockSpec((1, k, tile), lambda j, i: (layer, 0, j))],
        out_specs=pl.BlockSpec((tm, tile), lambda j, i: (i, j)),
        out_shape=jax.ShapeDtypeStruct((m, n_cols), _BF16),
        scratch_shapes=[pltpu.VMEM((k, tile), _BF16)],
        compiler_params=_cparams(("parallel", "arbitrary")),
        name="proj_matmul",
    )(x, w_stack)


def _rope_kernel(pos_ref, freq_ref, cos_ref, sin_ref):
    ang = pos_ref[0].astype(_F32) * freq_ref[...]
    lane = lax.broadcasted_iota(jnp.int32, ang.shape, 1)
    s = jnp.sin(ang)
    cos_ref[0] = jnp.cos(ang)
    sin_ref[0] = jnp.where(lane < QK_HEAD_DIM // 2, -s, s)


def _rope_tables(positions, ts=512):
    b, s = positions.shape
    half = QK_HEAD_DIM // 2
    inv_freq = (ROPE_BASE ** (-np.arange(half, dtype=np.float32) / np.float32(half))).astype(np.float32)
    freq = jnp.asarray(np.concatenate([inv_freq, inv_freq])[None, :])
    ts = min(ts, s)
    out = jax.ShapeDtypeStruct((b, s, QK_HEAD_DIM), _F32)
    return pl.pallas_call(
        _rope_kernel,
        grid=(b, s // ts),
        in_specs=[pl.BlockSpec((1, ts, 1), lambda i, j: (i, j, 0)),
                  pl.BlockSpec((1, QK_HEAD_DIM), lambda i, j: (0, 0))],
        out_specs=[pl.BlockSpec((1, ts, QK_HEAD_DIM), lambda i, j: (i, j, 0))] * 2,
        out_shape=[out, out],
        compiler_params=_cparams(("parallel", "parallel")),
        name="rope_tables",
    )(positions.reshape(b, s, 1), freq)


def _retention_kernel(lg_ref, q_ref, k_ref, v_ref, cos_ref, sin_ref, g_ref, o_ref, state_ref, decay_ref):
    c = q_ref.shape[1]
    lg = lg_ref[pl.program_id(1)]

    @pl.when(pl.program_id(2) == 0)
    def _():
        state_ref[...] = jnp.zeros_like(state_ref)
        ii = lax.broadcasted_iota(jnp.int32, (c, c), 0)
        jj = lax.broadcasted_iota(jnp.int32, (c, c), 1)
        rel = (ii - jj).astype(_F32)
        decay_ref[...] = jnp.where(rel >= 0, jnp.exp(lg * jnp.maximum(rel, 0.0)), 0.0)

    cos = cos_ref[0]
    sin = sin_ref[0]
    q = q_ref[0].astype(_F32)
    k = k_ref[0].astype(_F32)
    half = QK_HEAD_DIM // 2
    qr = q * cos + pltpu.roll(q, half, 1) * sin
    kr = (k * cos + pltpu.roll(k, half, 1) * sin) * (QK_HEAD_DIM ** -0.5)
    vb = v_ref[0]

    scores = _dot_nt(qr.astype(_BF16), kr.astype(_BF16)) * decay_ref[...]
    inner = _dot(scores.astype(_BF16), vb)

    idx = lax.broadcasted_iota(jnp.int32, (c, QK_HEAD_DIM), 0).astype(_F32)
    q_decay = jnp.exp((idx + 1.0) * lg)
    k_decay = jnp.exp((c - 1.0 - idx) * lg)
    state = state_ref[...]
    cross = _dot((qr * q_decay).astype(_BF16), state.astype(_BF16))
    state_ref[...] = state * jnp.exp(c * lg) + _dot_tn((kr * k_decay).astype(_BF16), vb)

    o = inner + cross
    mu = jnp.mean(o, -1, keepdims=True)
    d = o - mu
    var = jnp.mean(d * d, -1, keepdims=True)
    o_ref[0] = (d * lax.rsqrt(var + NORM_EPS) * g_ref[...]).astype(o_ref.dtype)


def _retention(h, cos, sin, norm_g, c=RET_CHUNK):
    b, s, _ = h.shape
    c = min(c, s)
    log_gamma = jnp.asarray(np.log1p(-np.exp2(-5.0 - np.arange(N_QK_HEADS, dtype=np.float64))).astype(np.float32))
    v_blk0 = 2 * QK_WIDTH // RET_V_HEAD_DIM
    grid_spec = pltpu.PrefetchScalarGridSpec(
        num_scalar_prefetch=1,
        grid=(b, N_QK_HEADS, s // c),
        in_specs=[
            pl.BlockSpec((1, c, QK_HEAD_DIM), lambda i, hd, t, lg: (i, t, hd)),
            pl.BlockSpec((1, c, QK_HEAD_DIM), lambda i, hd, t, lg: (i, t, N_QK_HEADS + hd)),
            pl.BlockSpec((1, c, RET_V_HEAD_DIM), lambda i, hd, t, lg: (i, t, v_blk0 + hd)),
            pl.BlockSpec((1, c, QK_HEAD_DIM), lambda i, hd, t, lg: (i, t, 0)),
            pl.BlockSpec((1, c, QK_HEAD_DIM), lambda i, hd, t, lg: (i, t, 0)),
            pl.BlockSpec((1, RET_V_HEAD_DIM), lambda i, hd, t, lg: (0, hd)),
        ],
        out_specs=pl.BlockSpec((1, c, RET_V_HEAD_DIM), lambda i, hd, t, lg: (i, t, hd)),
        scratch_shapes=[pltpu.VMEM((QK_HEAD_DIM, RET_V_HEAD_DIM), _F32),
                        pltpu.VMEM((c, c), _F32)],
    )
    return pl.pallas_call(
        _retention_kernel,
        grid_spec=grid_spec,
        out_shape=jax.ShapeDtypeStruct((b, s, TOK_WIDTH), _BF16),
        compiler_params=_cparams(("parallel", "parallel", "arbitrary")),
        name="retention",
    )(log_gamma, h, h, h, cos, sin, norm_g.reshape(1, TOK_WIDTH))


def _gdn_gate_kernel(x_ref, w_ref, alog_ref, dtb_ref, gc_ref, beta_ref):
    ab = _dot_nt(w_ref[...], x_ref[0])
    a = ab[:N_GDN_V_HEADS]
    bb = ab[N_GDN_V_HEADS:]
    g = -jnp.exp(alog_ref[...]) * jax.nn.softplus(a + dtb_ref[...])
    lane = lax.broadcasted_iota(jnp.int32, g.shape, 1) % GDN_CHUNK
    shift = 1
    while shift < GDN_CHUNK:
        g = g + jnp.where(lane >= shift, pltpu.roll(g, shift, 1), 0.0)
        shift *= 2
    gc_ref[0] = g
    beta_ref[0] = jax.nn.sigmoid(bb)


def _gdn_gates(xb, w_ab_t, a_log, dt_bias, ts=512):
    b, s, d = xb.shape
    ts = min(ts, s)
    out = jax.ShapeDtypeStruct((b, N_GDN_V_HEADS, s), _F32)
    return pl.pallas_call(
        _gdn_gate_kernel,
        grid=(b, s // ts),
        in_specs=[pl.BlockSpec((1, ts, d), lambda i, j: (i, j, 0)),
                  pl.BlockSpec((2 * N_GDN_V_HEADS, d), lambda i, j: (0, 0)),
                  pl.BlockSpec((N_GDN_V_HEADS, 1), lambda i, j: (0, 0)),
                  pl.BlockSpec((N_GDN_V_HEADS, 1), lambda i, j: (0, 0))],
        out_specs=[pl.BlockSpec((1, N_GDN_V_HEADS, ts), lambda i, j: (i, 0, j))] * 2,
        out_shape=[out, out],
        compiler_params=_cparams(("parallel", "parallel")),
        name="gdn_gates",
    )(xb, w_ab_t, a_log.reshape(-1, 1).astype(_F32), dt_bias.reshape(-1, 1).astype(_F32))


def _unit_lower_inverses(lows, ii, jj):
    n = lows[0].shape[0]
    xor = ii ^ jj
    eye = (ii == jj).astype(_F32)
    d1s = [jnp.where(xor < 8, low, 0.0) for low in lows]
    d2s = [_bdot(d1, d1) for d1 in d1s]
    yield None
    ts = [eye - d1 for d1 in d1s]
    ts = [t + _bdot(t, d2) for t, d2 in zip(ts, d2s)]
    yield None
    d4s = [_bdot(d2, d2) for d2 in d2s]
    yield None
    ts = [t + _bdot(t, d4) for t, d4 in zip(ts, d4s)]
    yield None
    size = 8
    while size < n:
        band = (xor >= size) & (xor < 2 * size)
        ys = [_bdot(t, jnp.where(band, low, 0.0)) for t, low in zip(ts, lows)]
        yield None
        ts = [t - _bdot(y, t) for t, y in zip(ts, ys)]
        yield None
        size *= 2
    yield ts


def _interleave(*stages):
    live = list(stages)
    while live:
        for stage in list(live):
            try:
                next(stage)
            except StopIteration:
                live.remove(stage)


def _causal_conv_silu(buf_ref, x, w):
    rows = x.shape[0]
    buf_ref[CONV_HALO:CONV_HALO + rows, :] = x
    y = x * w[CONV_WIDTH - 1:CONV_WIDTH, :]
    for lag in range(1, CONV_WIDTH):
        y = y + buf_ref[CONV_HALO - lag:CONV_HALO - lag + rows, :] * w[CONV_WIDTH - 1 - lag:CONV_WIDTH - lag, :]
    buf_ref[0:CONV_HALO, :] = buf_ref[rows:rows + CONV_HALO, :]
    return _silu(y)


def _gdn_kernel(blocks_per_seq, q_ref, k_ref, v_ref, wq_ref, wk_ref, wv_ref, gc_ref, beta_ref, ng_ref, o_ref,
                qbuf, kbuf, vbuf, state_ref, low_s, t_s, p_s, qg_s, kt_s, rhs_s, gl_s):
    rows = q_ref.shape[1]
    n = GDN_CHUNK
    dv = GDN_V_HEAD_DIM
    n_sys = 2 * (rows // n)
    g_step = pl.program_id(0)
    slot = g_step % 2

    @pl.when(g_step == 0)
    def _():
        for ref in (low_s, t_s, p_s, qg_s, kt_s, rhs_s, gl_s, state_ref):
            ref[...] = jnp.zeros_like(ref)

    @pl.when(g_step % blocks_per_seq == 0)
    def _():
        for ref in (qbuf, kbuf, vbuf):
            ref[0:CONV_HALO, :] = jnp.zeros((CONV_HALO, ref.shape[1]), _F32)

    ii = lax.broadcasted_iota(jnp.int32, (n, n), 0)
    jj = lax.broadcasted_iota(jnp.int32, (n, n), 1)
    causal = ii >= jj
    strict = ii > jj

    def recurrence_stage():
        fresh = (g_step - 2) % blocks_per_seq == 0
        ng = ng_ref[...]
        prods = []
        for idx in range(n_sys):
            wu = _dot(t_s[idx], rhs_s[slot, idx]).astype(_BF16)
            an = _dot_tn(kt_s[slot, idx], wu)
            pw = _dot(p_s[slot, idx], wu)
            prods.append((an[:, :n].astype(_BF16), an[:, n:],
                          (qg_s[slot, idx] - pw[:, :n]).astype(_BF16), pw[:, n:], gl_s[slot, idx, 0:1, :]))
            yield
        states = [jnp.where(fresh, 0.0, state_ref[e]) for e in range(2)]
        for idx, (a_mat, n_mat, q_eff, pu, gl) in enumerate(prods):
            c, e = divmod(idx, 2)
            sb = states[e].astype(_BF16)
            out = _dot(q_eff, sb) + pu
            states[e] = states[e] * gl - _dot(a_mat, sb) + n_mat
            out = out * lax.rsqrt(jnp.mean(out * out, -1, keepdims=True) + NORM_EPS) * ng
            o_ref[0, c * n:(c + 1) * n, e * dv:(e + 1) * dv] = out.astype(o_ref.dtype)
            yield
        state_ref[0] = states[0]
        state_ref[1] = states[1]

    def inverse_stage():
        for t_invs in _unit_lower_inverses([low_s[idx] for idx in range(n_sys)], ii, jj):
            yield
        for idx, t_inv in enumerate(t_invs):
            t_s[idx] = t_inv.astype(_BF16)

    def operand_stage():
        q_all = _causal_conv_silu(qbuf, q_ref[0].astype(_F32), wq_ref[...])
        yield
        k_all = _causal_conv_silu(kbuf, k_ref[0].astype(_F32), wk_ref[...])
        yield
        v_all = _causal_conv_silu(vbuf, v_ref[0].astype(_F32), wv_ref[...])
        yield
        for c in range(rows // n):
            r0 = c * n
            qc = q_all[r0:r0 + n]
            kc = k_all[r0:r0 + n]
            qn = qc * (lax.rsqrt(jnp.sum(qc * qc, -1, keepdims=True) + NORM_EPS) * (QK_HEAD_DIM ** -0.5))
            kn = kc * lax.rsqrt(jnp.sum(kc * kc, -1, keepdims=True) + NORM_EPS)
            kb = kn.astype(_BF16)
            kk = _dot_nt(kb, kb)
            qk = _dot_nt(qn.astype(_BF16), kb)
            for e in range(2):
                idx = 2 * c + e
                g_row = jnp.broadcast_to(gc_ref[0, 0, e:e + 1, r0:r0 + n], (n, n))
                g_col = g_row.T
                b_col = jnp.broadcast_to(beta_ref[0, 0, e:e + 1, r0:r0 + n], (n, n)).T
                decay = jnp.where(causal, jnp.exp(jnp.where(causal, g_col - g_row, 0.0)), 0.0)
                e_g = jnp.exp(g_col)
                g_last = g_col[n - 1:n, :]
                v_e = v_all[r0:r0 + n, e * dv:(e + 1) * dv]
                low_s[idx] = jnp.where(strict, b_col * kk * decay, 0.0)
                p_s[slot, idx] = (qk * decay).astype(_BF16)
                qg_s[slot, idx] = qn * e_g
                kt_s[slot, idx] = (kn * jnp.exp(g_last - g_col)).astype(_BF16)
                rhs_s[slot, idx] = jnp.concatenate([b_col * e_g * kn, b_col * v_e], 1).astype(_BF16)
                gl_s[slot, idx] = jnp.broadcast_to(jnp.exp(g_last), (CONV_HALO, dv))
                yield

    _interleave(recurrence_stage(), inverse_stage(), operand_stage())


def _gdn(h, conv_w, gc, beta, norm_g, rows=GDN_ROWS):
    b, s, _ = h.shape
    rows = min(rows, s)
    nb = s // rows
    total = b * N_QK_HEADS * nb
    v_blk0 = 2 * QK_WIDTH // (2 * GDN_V_HEAD_DIM)
    gc = gc.reshape(b, N_QK_HEADS, 2, s)
    beta = beta.reshape(b, N_QK_HEADS, 2, s)
    pair = 2 * GDN_V_HEAD_DIM
    n_sys = 2 * (rows // GDN_CHUNK)

    def where(g):
        g = jnp.minimum(g, total - 1)
        return g // (N_QK_HEADS * nb), (g // nb) % N_QK_HEADS, g % nb

    def cur(col0):
        def index(g):
            i, hd, t = where(g)
            return i, t, col0 + hd
        return index

    def gate_index(g):
        i, hd, t = where(g)
        return i, hd, 0, t

    def out_index(g):
        i, hd, t = where(jnp.maximum(g - 2, 0))
        return i, t, hd

    return pl.pallas_call(
        functools.partial(_gdn_kernel, nb),
        grid=(total + 2,),
        in_specs=[
            pl.BlockSpec((1, rows, QK_HEAD_DIM), cur(0)),
            pl.BlockSpec((1, rows, QK_HEAD_DIM), cur(N_QK_HEADS)),
            pl.BlockSpec((1, rows, pair), cur(v_blk0)),
            pl.BlockSpec((CONV_WIDTH, QK_HEAD_DIM), lambda g: (0, where(g)[1])),
            pl.BlockSpec((CONV_WIDTH, QK_HEAD_DIM), lambda g: (0, N_QK_HEADS + where(g)[1])),
            pl.BlockSpec((CONV_WIDTH, pair), lambda g: (0, v_blk0 + where(g)[1])),
            pl.BlockSpec((1, 1, 2, rows), gate_index),
            pl.BlockSpec((1, 1, 2, rows), gate_index),
            pl.BlockSpec((1, GDN_V_HEAD_DIM), lambda g: (0, 0)),
        ],
        out_specs=pl.BlockSpec((1, rows, pair), out_index),
        out_shape=jax.ShapeDtypeStruct((b, s, TOK_WIDTH), _BF16),
        scratch_shapes=[pltpu.VMEM((rows + CONV_HALO, QK_HEAD_DIM), _F32),
                        pltpu.VMEM((rows + CONV_HALO, QK_HEAD_DIM), _F32),
                        pltpu.VMEM((rows + CONV_HALO, pair), _F32),
                        pltpu.VMEM((2, QK_HEAD_DIM, GDN_V_HEAD_DIM), _F32),
                        pltpu.VMEM((n_sys, GDN_CHUNK, GDN_CHUNK), _F32),
                        pltpu.VMEM((n_sys, GDN_CHUNK, GDN_CHUNK), _BF16),
                        pltpu.VMEM((2, n_sys, GDN_CHUNK, GDN_CHUNK), _BF16),
                        pltpu.VMEM((2, n_sys, GDN_CHUNK, QK_HEAD_DIM), _F32),
                        pltpu.VMEM((2, n_sys, GDN_CHUNK, QK_HEAD_DIM), _BF16),
                        pltpu.VMEM((2, n_sys, GDN_CHUNK, QK_HEAD_DIM + GDN_V_HEAD_DIM), _BF16),
                        pltpu.VMEM((2, n_sys, CONV_HALO, GDN_V_HEAD_DIM), _F32)],
        compiler_params=_cparams(("arbitrary",)),
        name="gated_delta",
    )(h, h, h, conv_w, conv_w, conv_w, gc, beta, norm_g.reshape(1, GDN_V_HEAD_DIM))


def _out_kernel(tok_ref, mq_ref, z0_ref, z1_ref, z2_ref, z3_ref, kv_ref, w_ref, x_ref, g_ref, b_ref,
                o_ref, ob_ref, br_ref):
    t = PROJ_TILE
    for blk, z_ref in enumerate((z0_ref, z1_ref, z2_ref)):
        tok = tok_ref[:, blk * t:(blk + 1) * t].astype(_F32)
        br_ref[:, blk * t:(blk + 1) * t] = (tok * _silu(z_ref[...].astype(_F32))).astype(_BF16)
    for m in range(MEM_HEADS):
        lo = m * MEM_HEAD_DIM
        mk = kv_ref[0, :, lo:lo + MEM_HEAD_DIM]
        mv = kv_ref[0, :, MEM_WIDTH + lo:MEM_WIDTH + lo + MEM_HEAD_DIM]
        sc = _dot_nt(mq_ref[:, lo:lo + MEM_HEAD_DIM], mk) * (MEM_HEAD_DIM ** -0.5)
        p = jnp.exp(sc - jnp.max(sc, -1, keepdims=True))
        p = p / jnp.sum(p, -1, keepdims=True)
        att = _dot(p.astype(_BF16), mv)
        z = z3_ref[:, lo:lo + MEM_HEAD_DIM].astype(_F32)
        br_ref[:, TOK_WIDTH + lo:TOK_WIDTH + lo + MEM_HEAD_DIM] = (att * _silu(z)).astype(_BF16)

    y = _dot(br_ref[...], w_ref[0])
    r = DEEPNORM_ALPHA * x_ref[...] + y
    mu = jnp.mean(r, -1, keepdims=True)
    d = r - mu
    var = jnp.mean(d * d, -1, keepdims=True)
    out = d * lax.rsqrt(var + LN_EPS) * g_ref[...] + b_ref[...]
    o_ref[...] = out
    ob_ref[...] = out.astype(_BF16)


def _gate_out_ln(tok, h, kv, w_out_b, layer, x, ln_g, ln_b, seq, tm=OUT_ROWS):
    m, d = x.shape
    tm = min(tm, seq)
    n_mem = kv.shape[1]
    z_blk0 = (CONV_CH + MEM_WIDTH) // PROJ_TILE
    once = pl.Buffered(1)

    def h_cols(blk):
        return pl.BlockSpec((tm, PROJ_TILE), lambda i: (i, blk))

    return pl.pallas_call(
        _out_kernel,
        grid=(m // tm,),
        in_specs=[pl.BlockSpec((tm, TOK_WIDTH), lambda i: (i, 0)),
                  h_cols(CONV_CH // PROJ_TILE),
                  h_cols(z_blk0), h_cols(z_blk0 + 1), h_cols(z_blk0 + 2), h_cols(z_blk0 + 3),
                  pl.BlockSpec((1, n_mem, 2 * MEM_WIDTH), lambda i: (i * tm // seq, 0, 0)),
                  pl.BlockSpec((1, MIX_WIDTH, d), lambda i: (layer, 0, 0), pipeline_mode=once),
                  pl.BlockSpec((tm, d), lambda i: (i, 0)),
                  pl.BlockSpec((1, d), lambda i: (0, 0)),
                  pl.BlockSpec((1, d), lambda i: (0, 0))],
        out_specs=[pl.BlockSpec((tm, d), lambda i: (i, 0))] * 2,
        out_shape=[jax.ShapeDtypeStruct((m, d), _F32), jax.ShapeDtypeStruct((m, d), _BF16)],
        scratch_shapes=[pltpu.VMEM((tm, MIX_WIDTH), _BF16)],
        compiler_params=_cparams(("parallel",)),
        name="gate_out_layernorm",
    )(tok, h, h, h, h, h, kv, w_out_b, x, ln_g.reshape(1, d), ln_b.reshape(1, d))


def kernel(x, mem, positions, w_in_ret, ret_norm_g, w_in_gdn, conv_w, a_log, dt_bias, gdn_norm_g,
           w_mem_kv, w_out, ln_g, ln_b):
    b, s, d = x.shape
    m = b * s
    n_mem = mem.shape[1]
    cos, sin = _rope_tables(positions)
    mem_b = mem.reshape(b * n_mem, d).astype(_BF16)
    w_out_b = w_out.astype(_BF16)
    xf = x.reshape(m, d)
    xb = xf.astype(_BF16)
    for i in range(DEPTH):
        j = i // 2
        if i % 2 == 0:
            h = _project(xb, w_in_ret, j, RET_COLS)
            tok = _retention(h.reshape(b, s, RET_COLS), cos, sin, ret_norm_g[j])
        else:
            h = _project(xb, w_in_gdn, j, RET_COLS)
            w_ab_t = w_in_gdn[j, :, RET_COLS:].T.astype(_BF16)
            gc, beta = _gdn_gates(xb.reshape(b, s, d), w_ab_t, a_log[j], dt_bias[j])
            tok = _gdn(h.reshape(b, s, RET_COLS), conv_w[j], gc, beta, gdn_norm_g[j])
        kv = _project(mem_b, w_mem_kv, i, 2 * MEM_WIDTH).reshape(b, n_mem, 2 * MEM_WIDTH)
        xf, xb = _gate_out_ln(tok.reshape(m, TOK_WIDTH), h, kv, w_out_b, i, xf, ln_g[i], ln_b[i], s)
    return xf.reshape(b, s, d)
```

```python
import functools

import numpy as np
import jax
import jax.numpy as jnp
from jax import lax
from jax.experimental import pallas as pl
from jax.experimental.pallas import tpu as pltpu

QK_HEAD_DIM = 128
N_QK_HEADS = 12
QK_WIDTH = N_QK_HEADS * QK_HEAD_DIM
TOK_WIDTH = 3072
RET_V_HEAD_DIM = 256
GDN_V_HEAD_DIM = 128
N_GDN_V_HEADS = 24
MEM_HEADS = 4
MEM_HEAD_DIM = 256
MEM_WIDTH = MEM_HEADS * MEM_HEAD_DIM
MIX_WIDTH = TOK_WIDTH + MEM_WIDTH
CONV_WIDTH = 4
CONV_CH = 2 * QK_WIDTH + TOK_WIDTH
RET_COLS = CONV_CH + MEM_WIDTH + MIX_WIDTH
ROPE_BASE = 10000.0
DEPTH = 4
DEEPNORM_ALPHA = (2.0 * DEPTH) ** 0.25
LN_EPS = 1e-5
NORM_EPS = 1e-6

V7X_VMEM_BYTES = 64 * 1024 * 1024
VMEM_LIMIT = V7X_VMEM_BYTES - 8 * 1024 * 1024
LANES = 128

PROJ_TILE = 1024
RET_CHUNK = 512
RET_HEADS_PER_STEP = 2
GDN_CHUNK = 128
GDN_ROWS = 512
CONV_HALO = 8
OUT_ROWS = 256
OUT_K_SPLIT = 8

_F32 = jnp.float32
_BF16 = jnp.bfloat16


def _cparams(sem):
    return pltpu.CompilerParams(dimension_semantics=sem, vmem_limit_bytes=VMEM_LIMIT)


def _dot(a, b):
    return jnp.dot(a, b, preferred_element_type=_F32)


def _bdot(a, b):
    return _dot(a.astype(_BF16), b.astype(_BF16))


def _dot_nt(a, b):
    return lax.dot_general(a, b, (((1,), (1,)), ((), ())), preferred_element_type=_F32)


def _dot_tn(a, b):
    return lax.dot_general(a, b, (((0,), (0,)), ((), ())), preferred_element_type=_F32)


def _silu(z):
    return z * jax.nn.sigmoid(z)


def _interleave(*stages):
    live = [[gen, count, 0] for gen, count in stages]
    while live:
        stage = min(live, key=lambda st: (st[2] + 1) / st[1])
        try:
            next(stage[0])
            stage[2] += 1
        except StopIteration:
            live.remove(stage)


def _proj_kernel(x_ref, w_ref, o_ref, wb_ref):
    @pl.when(pl.program_id(1) == 0)
    def _():
        wb_ref[...] = w_ref[0].astype(_BF16)

    o_ref[...] = _dot(x_ref[...], wb_ref[...]).astype(o_ref.dtype)


def _project(x, w_stack, layer, n_cols, tile=PROJ_TILE):
    m, k = x.shape
    tm = min(tile, m)
    assert m % tm == 0 and n_cols % tile == 0
    return pl.pallas_call(
        _proj_kernel,
        grid=(n_cols // tile, m // tm),
        in_specs=[pl.BlockSpec((tm, k), lambda j, i: (i, 0)),
                  pl.BlockSpec((1, k, tile), lambda j, i: (layer, 0, j))],
        out_specs=pl.BlockSpec((tm, tile), lambda j, i: (i, j)),
        out_shape=jax.ShapeDtypeStruct((m, n_cols), _BF16),
        scratch_shapes=[pltpu.VMEM((k, tile), _BF16)],
        compiler_params=_cparams(("parallel", "arbitrary")),
        name="proj_matmul",
    )(x, w_stack)


def _rope_kernel(pos_ref, freq_ref, cos_ref, sin_ref):
    ang = pos_ref[0].astype(_F32) * freq_ref[...]
    lane = lax.broadcasted_iota(jnp.int32, ang.shape, 1)
    s = jnp.sin(ang)
    cos_ref[0] = jnp.cos(ang)
    sin_ref[0] = jnp.where(lane < QK_HEAD_DIM // 2, -s, s)


def _rope_tables(positions, ts=512):
    b, s = positions.shape
    half = QK_HEAD_DIM // 2
    inv_freq = (ROPE_BASE ** (-np.arange(half, dtype=np.float32) / np.float32(half))).astype(np.float32)
    freq = jnp.asarray(np.concatenate([inv_freq, inv_freq])[None, :])
    ts = min(ts, s)
    out = jax.ShapeDtypeStruct((b, s, QK_HEAD_DIM), _F32)
    return pl.pallas_call(
        _rope_kernel,
        grid=(b, s // ts),
        in_specs=[pl.BlockSpec((1, ts, 1), lambda i, j: (i, j, 0)),
                  pl.BlockSpec((1, QK_HEAD_DIM), lambda i, j: (0, 0))],
        out_specs=[pl.BlockSpec((1, ts, QK_HEAD_DIM), lambda i, j: (i, j, 0))] * 2,
        out_shape=[out, out],
        compiler_params=_cparams(("parallel", "parallel")),
        name="rope_tables",
    )(positions.reshape(b, s, 1), freq)


def _retention_kernel(lg_ref, q_ref, k_ref, v_ref, cos_ref, sin_ref, g_ref, o_ref, state_ref, decay_ref):
    c = q_ref.shape[1]
    dk, dv = QK_HEAD_DIM, RET_V_HEAD_DIM
    pair = pl.program_id(1)

    @pl.when(pl.program_id(2) == 0)
    def _():
        state_ref[...] = jnp.zeros_like(state_ref)
        ii = lax.broadcasted_iota(jnp.int32, (c, c), 0)
        jj = lax.broadcasted_iota(jnp.int32, (c, c), 1)
        rel = (ii - jj).astype(_F32)
        for j in range(RET_HEADS_PER_STEP):
            lg = lg_ref[RET_HEADS_PER_STEP * pair + j]
            decay_ref[j] = jnp.where(rel >= 0, jnp.exp(lg * jnp.maximum(rel, 0.0)), 0.0)

    cos = cos_ref[0]
    sin = sin_ref[0]
    idx = lax.broadcasted_iota(jnp.int32, (c, dk), 0).astype(_F32)

    def head(j):
        lg = lg_ref[RET_HEADS_PER_STEP * pair + j]
        q = q_ref[0, :, j * dk:(j + 1) * dk].astype(_F32)
        k = k_ref[0, :, j * dk:(j + 1) * dk].astype(_F32)
        qr = q * cos + pltpu.roll(q, dk // 2, 1) * sin
        kr = (k * cos + pltpu.roll(k, dk // 2, 1) * sin) * (dk ** -0.5)
        qb = qr.astype(_BF16)
        kb = kr.astype(_BF16)
        q_in = (qr * jnp.exp((idx + 1.0) * lg)).astype(_BF16)
        k_out = (kr * jnp.exp((c - 1.0 - idx) * lg)).astype(_BF16)
        yield
        scores = (_dot_nt(qb, kb) * decay_ref[j]).astype(_BF16)
        yield
        vb = v_ref[0, :, j * dv:(j + 1) * dv]
        state = state_ref[j]
        o = _dot(jnp.concatenate([scores, q_in], 1), jnp.concatenate([vb, state.astype(_BF16)], 0))
        state_ref[j] = state * jnp.exp(c * lg) + _dot_tn(k_out, vb)
        yield
        mu = jnp.mean(o, -1, keepdims=True)
        d = o - mu
        var = jnp.mean(d * d, -1, keepdims=True)
        o_ref[0, :, j * dv:(j + 1) * dv] = (
            d * lax.rsqrt(var + NORM_EPS) * g_ref[:, j * dv:(j + 1) * dv]).astype(o_ref.dtype)

    _interleave(*[(head(j), 4) for j in range(RET_HEADS_PER_STEP)])


def _retention(h, cos, sin, norm_g, c=RET_CHUNK):
    b, s, _ = h.shape
    c = min(c, s)
    hp = RET_HEADS_PER_STEP
    log_gamma = jnp.asarray(np.log1p(-np.exp2(-5.0 - np.arange(N_QK_HEADS, dtype=np.float64))).astype(np.float32))
    n_pairs = N_QK_HEADS // hp
    v_blk0 = 2 * QK_WIDTH // (hp * RET_V_HEAD_DIM)
    grid_spec = pltpu.PrefetchScalarGridSpec(
        num_scalar_prefetch=1,
        grid=(b, n_pairs, s // c),
        in_specs=[
            pl.BlockSpec((1, c, hp * QK_HEAD_DIM), lambda i, hd, t, lg: (i, t, hd)),
            pl.BlockSpec((1, c, hp * QK_HEAD_DIM), lambda i, hd, t, lg: (i, t, n_pairs + hd)),
            pl.BlockSpec((1, c, hp * RET_V_HEAD_DIM), lambda i, hd, t, lg: (i, t, v_blk0 + hd)),
            pl.BlockSpec((1, c, QK_HEAD_DIM), lambda i, hd, t, lg: (i, t, 0)),
            pl.BlockSpec((1, c, QK_HEAD_DIM), lambda i, hd, t, lg: (i, t, 0)),
            pl.BlockSpec((1, hp * RET_V_HEAD_DIM), lambda i, hd, t, lg: (0, hd)),
        ],
        out_specs=pl.BlockSpec((1, c, hp * RET_V_HEAD_DIM), lambda i, hd, t, lg: (i, t, hd)),
        scratch_shapes=[pltpu.VMEM((hp, QK_HEAD_DIM, RET_V_HEAD_DIM), _F32),
                        pltpu.VMEM((hp, c, c), _F32)],
    )
    return pl.pallas_call(
        _retention_kernel,
        grid_spec=grid_spec,
        out_shape=jax.ShapeDtypeStruct((b, s, TOK_WIDTH), _BF16),
        compiler_params=_cparams(("parallel", "parallel", "arbitrary")),
        name="retention",
    )(log_gamma, h, h, h, cos, sin, norm_g.reshape(1, TOK_WIDTH))


def _gdn_gate_kernel(x_ref, w_ref, alog_ref, dtb_ref, gc_ref, beta_ref):
    ab = _dot_nt(w_ref[...].T.astype(_BF16), x_ref[0])
    a = ab[:N_GDN_V_HEADS]
    bb = ab[N_GDN_V_HEADS:2 * N_GDN_V_HEADS]
    g = -jnp.exp(alog_ref[...]) * jax.nn.softplus(a + dtb_ref[...])
    lane = lax.broadcasted_iota(jnp.int32, g.shape, 1) % GDN_CHUNK
    shift = 1
    while shift < GDN_CHUNK:
        g = g + jnp.where(lane >= shift, pltpu.roll(g, shift, 1), 0.0)
        shift *= 2
    gc_ref[0] = g
    beta_ref[0] = jax.nn.sigmoid(bb)


def _gdn_gates(xb, w_ab, a_log, dt_bias, ts=512):
    b, s, d = xb.shape
    ts = min(ts, s)
    w_ab = jnp.pad(w_ab, ((0, 0), (0, LANES - w_ab.shape[1])))
    out = jax.ShapeDtypeStruct((b, N_GDN_V_HEADS, s), _F32)
    return pl.pallas_call(
        _gdn_gate_kernel,
        grid=(b, s // ts),
        in_specs=[pl.BlockSpec((1, ts, d), lambda i, j: (i, j, 0)),
                  pl.BlockSpec((d, LANES), lambda i, j: (0, 0)),
                  pl.BlockSpec((N_GDN_V_HEADS, 1), lambda i, j: (0, 0)),
                  pl.BlockSpec((N_GDN_V_HEADS, 1), lambda i, j: (0, 0))],
        out_specs=[pl.BlockSpec((1, N_GDN_V_HEADS, ts), lambda i, j: (i, 0, j))] * 2,
        out_shape=[out, out],
        compiler_params=_cparams(("parallel", "parallel")),
        name="gdn_gates",
    )(xb, w_ab, a_log.reshape(-1, 1).astype(_F32), dt_bias.reshape(-1, 1).astype(_F32))


def _unit_lower_inverses(lows, ii, jj):
    n = lows[0].shape[0]
    xor = ii ^ jj
    eye = (ii == jj).astype(_F32)
    d1s = [jnp.where(xor < 8, low, 0.0) for low in lows]
    d1bs = [d1.astype(_BF16) for d1 in d1s]
    d2bs = [_dot(d1b, d1b).astype(_BF16) for d1b in d1bs]
    yield None
    ts = [eye - d1 for d1 in d1s]
    ts = [t + _dot(t.astype(_BF16), d2b) for t, d2b in zip(ts, d2bs)]
    yield None
    d4bs = [_dot(d2b, d2b).astype(_BF16) for d2b in d2bs]
    yield None
    ts = [t + _dot(t.astype(_BF16), d4b) for t, d4b in zip(ts, d4bs)]
    yield None
    neg_lows = [(-low).astype(_BF16) for low in lows]
    size = 8
    while size < n:
        band = (xor >= size) & (xor < 2 * size)
        tbs = [t.astype(_BF16) for t in ts]
        ys = [_dot(tb, nl).astype(_BF16) for tb, nl in zip(tbs, neg_lows)]
        yield None
        ts = [jnp.where(band, _dot(y, tb), t) for t, y, tb in zip(ts, ys, tbs)]
        yield None
        size *= 2
    yield ts


def _conv_silu_rows(buf_ref, w, r0, n):
    y = None
    for lag in range(CONV_WIDTH):
        lo = CONV_HALO + r0 - lag
        term = buf_ref[lo:lo + n, :] * w[CONV_WIDTH - 1 - lag:CONV_WIDTH - lag, :]
        y = term if y is None else y + term
    return _silu(y)


def _gdn_kernel(blocks_per_seq, q_ref, k_ref, v_ref, wq_ref, wk_ref, wv_ref, gc_ref, beta_ref, ng_ref, o_ref,
                qbuf, kbuf, vbuf, state_ref, low_s, t_s, kp_s, qg_s, rhs_s, gl_s):
    rows = q_ref.shape[1]
    n = GDN_CHUNK
    dv = GDN_V_HEAD_DIM
    n_sys = 2 * (rows // n)
    g_step = pl.program_id(0)
    slot = g_step % 2

    @pl.when(g_step == 0)
    def _():
        for ref in (low_s, t_s, kp_s, qg_s, rhs_s, gl_s, state_ref):
            ref[...] = jnp.zeros_like(ref)

    @pl.when(g_step % blocks_per_seq == 0)
    def _():
        for ref in (qbuf, kbuf, vbuf):
            ref[0:CONV_HALO, :] = jnp.zeros((CONV_HALO, ref.shape[1]), _F32)

    ii = lax.broadcasted_iota(jnp.int32, (n, n), 0)
    jj = lax.broadcasted_iota(jnp.int32, (n, n), 1)
    causal = ii >= jj
    strict = ii > jj

    def recurrence_stage():
        fresh = (g_step - 2) % blocks_per_seq == 0
        ng = ng_ref[...]
        prods = []
        for idx in range(n_sys):
            wu = _dot(t_s[idx], rhs_s[slot, idx]).astype(_BF16)
            both = _dot(kp_s[slot, idx], wu)
            q_eff = (qg_s[slot, idx] - both[n:, :n]).astype(_BF16)
            prods.append((jnp.concatenate([q_eff, both[:n, :n].astype(_BF16)], 0), both[:n, n:],
                          both[n:, n:], gl_s[slot, idx, 0:1, :]))
            yield
        states = [jnp.where(fresh, 0.0, state_ref[e]) for e in range(2)]
        for idx, (qa, n_mat, pu, gl) in enumerate(prods):
            c, e = divmod(idx, 2)
            qa_s = _dot(qa, states[e].astype(_BF16))
            out = qa_s[:n] + pu
            states[e] = states[e] * gl - qa_s[n:] + n_mat
            out = out * lax.rsqrt(jnp.mean(out * out, -1, keepdims=True) + NORM_EPS) * ng
            o_ref[0, c * n:(c + 1) * n, e * dv:(e + 1) * dv] = out.astype(o_ref.dtype)
            yield
        state_ref[0] = states[0]
        state_ref[1] = states[1]

    def inverse_stage():
        for t_invs in _unit_lower_inverses([low_s[idx] for idx in range(n_sys)], ii, jj):
            yield
        for idx, t_inv in enumerate(t_invs):
            t_s[idx] = t_inv.astype(_BF16)

    def operand_stage():
        bufs = ((qbuf, q_ref), (kbuf, k_ref), (vbuf, v_ref))
        for buf, ref in bufs:
            buf[CONV_HALO:CONV_HALO + rows, :] = ref[0].astype(_F32)
        yield
        wq, wk, wv = wq_ref[...], wk_ref[...], wv_ref[...]
        for c in range(rows // n):
            r0 = c * n
            qc = _conv_silu_rows(qbuf, wq, r0, n)
            kc = _conv_silu_rows(kbuf, wk, r0, n)
            qn = qc * (lax.rsqrt(jnp.sum(qc * qc, -1, keepdims=True) + NORM_EPS) * (QK_HEAD_DIM ** -0.5))
            kn = kc * lax.rsqrt(jnp.sum(kc * kc, -1, keepdims=True) + NORM_EPS)
            kb = kn.astype(_BF16)
            kq = _dot_nt(jnp.concatenate([kb, qn.astype(_BF16)], 0), kb)
            kk, qk = kq[:n], kq[n:]
            yield
            vc = _conv_silu_rows(vbuf, wv, r0, n)
            for e in range(2):
                idx = 2 * c + e
                g_row = jnp.broadcast_to(gc_ref[0, 0, e:e + 1, r0:r0 + n], (n, n))
                g_col = g_row.T
                b_col = jnp.broadcast_to(beta_ref[0, 0, e:e + 1, r0:r0 + n], (n, n)).T
                decay = jnp.where(causal, jnp.exp(jnp.where(causal, g_col - g_row, 0.0)), 0.0)
                e_g = jnp.exp(g_col)
                g_last = g_col[n - 1:n, :]
                v_e = vc[:, e * dv:(e + 1) * dv]
                low_s[idx] = jnp.where(strict, b_col * kk * decay, 0.0)
                k_tail_t = (kn * jnp.exp(g_last - g_col)).T
                kp_s[slot, idx] = jnp.concatenate([k_tail_t, qk * decay], 0).astype(_BF16)
                qg_s[slot, idx] = qn * e_g
                rhs_s[slot, idx] = jnp.concatenate([b_col * e_g * kn, b_col * v_e], 1).astype(_BF16)
                gl_s[slot, idx] = jnp.broadcast_to(jnp.exp(g_last), (CONV_HALO, dv))
                yield
        for buf, _ in bufs:
            buf[0:CONV_HALO, :] = buf[rows:rows + CONV_HALO, :]

    n_chunks = rows // n
    _interleave((recurrence_stage(), 2 * n_sys + 1), (inverse_stage(), 13), (operand_stage(), 3 * n_chunks + 2))


def _gdn(h, conv_w, gc, beta, norm_g, rows=GDN_ROWS):
    b, s, _ = h.shape
    rows = min(rows, s)
    nb = s // rows
    total = b * N_QK_HEADS * nb
    v_blk0 = 2 * QK_WIDTH // (2 * GDN_V_HEAD_DIM)
    gc = gc.reshape(b, N_QK_HEADS, 2, s)
    beta = beta.reshape(b, N_QK_HEADS, 2, s)
    pair = 2 * GDN_V_HEAD_DIM
    n_sys = 2 * (rows // GDN_CHUNK)

    def where(g):
        g = jnp.minimum(g, total - 1)
        return g // (N_QK_HEADS * nb), (g // nb) % N_QK_HEADS, g % nb

    def cur(col0):
        def index(g):
            i, hd, t = where(g)
            return i, t, col0 + hd
        return index

    def gate_index(g):
        i, hd, t = where(g)
        return i, hd, 0, t

    def out_index(g):
        i, hd, t = where(jnp.maximum(g - 2, 0))
        return i, t, hd

    return pl.pallas_call(
        functools.partial(_gdn_kernel, nb),
        grid=(total + 2,),
        in_specs=[
            pl.BlockSpec((1, rows, QK_HEAD_DIM), cur(0)),
            pl.BlockSpec((1, rows, QK_HEAD_DIM), cur(N_QK_HEADS)),
            pl.BlockSpec((1, rows, pair), cur(v_blk0)),
            pl.BlockSpec((CONV_WIDTH, QK_HEAD_DIM), lambda g: (0, where(g)[1])),
            pl.BlockSpec((CONV_WIDTH, QK_HEAD_DIM), lambda g: (0, N_QK_HEADS + where(g)[1])),
            pl.BlockSpec((CONV_WIDTH, pair), lambda g: (0, v_blk0 + where(g)[1])),
            pl.BlockSpec((1, 1, 2, rows), gate_index),
            pl.BlockSpec((1, 1, 2, rows), gate_index),
            pl.BlockSpec((1, GDN_V_HEAD_DIM), lambda g: (0, 0)),
        ],
        out_specs=pl.BlockSpec((1, rows, pair), out_index),
        out_shape=jax.ShapeDtypeStruct((b, s, TOK_WIDTH), _BF16),
        scratch_shapes=[pltpu.VMEM((rows + CONV_HALO, QK_HEAD_DIM), _F32),
                        pltpu.VMEM((rows + CONV_HALO, QK_HEAD_DIM), _F32),
                        pltpu.VMEM((rows + CONV_HALO, pair), _F32),
                        pltpu.VMEM((2, QK_HEAD_DIM, GDN_V_HEAD_DIM), _F32),
                        pltpu.VMEM((n_sys, GDN_CHUNK, GDN_CHUNK), _F32),
                        pltpu.VMEM((n_sys, GDN_CHUNK, GDN_CHUNK), _BF16),
                        pltpu.VMEM((2, n_sys, QK_HEAD_DIM + GDN_CHUNK, GDN_CHUNK), _BF16),
                        pltpu.VMEM((2, n_sys, GDN_CHUNK, QK_HEAD_DIM), _F32),
                        pltpu.VMEM((2, n_sys, GDN_CHUNK, QK_HEAD_DIM + GDN_V_HEAD_DIM), _BF16),
                        pltpu.VMEM((2, n_sys, CONV_HALO, GDN_V_HEAD_DIM), _F32)],
        compiler_params=_cparams(("arbitrary",)),
        name="gated_delta",
    )(h, h, h, conv_w, conv_w, conv_w, gc, beta, norm_g.reshape(1, GDN_V_HEAD_DIM))


def _out_kernel(tok_ref, mq_ref, z0_ref, z1_ref, z2_ref, z3_ref, kv_ref, w_ref, x_ref, g_ref, b_ref,
                o_ref, ob_ref, y_s):
    tm, d = x_ref.shape
    kc = MIX_WIDTH // OUT_K_SPLIT
    z_refs = (z0_ref, z1_ref, z2_ref, z3_ref)

    @pl.when(pl.program_id(0) == 0)
    def _():
        y_s[...] = jnp.zeros_like(y_s)

    def attend(m):
        lo = m * MEM_HEAD_DIM
        mk = kv_ref[0, :, lo:lo + MEM_HEAD_DIM]
        mv = kv_ref[0, :, MEM_WIDTH + lo:MEM_WIDTH + lo + MEM_HEAD_DIM]
        sc = _dot_nt(mq_ref[:, lo:lo + MEM_HEAD_DIM], mk) * (MEM_HEAD_DIM ** -0.5)
        p = jnp.exp(sc - jnp.max(sc, -1, keepdims=True))
        p = p / jnp.sum(p, -1, keepdims=True)
        return _dot(p.astype(_BF16), mv)

    def branch_chunk(k):
        lo = k * kc
        z = z_refs[lo // PROJ_TILE][:, lo % PROJ_TILE:lo % PROJ_TILE + kc].astype(_F32)
        if lo < TOK_WIDTH:
            val = tok_ref[:, lo:lo + kc].astype(_F32)
        else:
            first = (lo - TOK_WIDTH) // MEM_HEAD_DIM
            val = jnp.concatenate([attend(first + j) for j in range(kc // MEM_HEAD_DIM)], 1)
        return (val * _silu(z)).astype(_BF16)

    def project_stage():
        acc = None
        for k in range(OUT_K_SPLIT):
            part = _dot(branch_chunk(k), w_ref[0, k * kc:(k + 1) * kc, :])
            acc = part if acc is None else acc + part
            yield
        y_s[...] = acc

    def norm_stage():
        rows = tm // OUT_K_SPLIT
        for blk in range(OUT_K_SPLIT):
            sl = slice(blk * rows, (blk + 1) * rows)
            r = DEEPNORM_ALPHA * x_ref[sl, :] + y_s[sl, :]
            mu = jnp.mean(r, -1, keepdims=True)
            dev = r - mu
            var = jnp.mean(dev * dev, -1, keepdims=True)
            out = dev * lax.rsqrt(var + LN_EPS) * g_ref[...] + b_ref[...]
            o_ref[sl, :] = out
            ob_ref[sl, :] = out.astype(_BF16)
            yield

    _interleave((norm_stage(), OUT_K_SPLIT), (project_stage(), OUT_K_SPLIT + 1))


def _gate_out_ln(tok, h, kv, w_out_b, layer, x, ln_g, ln_b, seq, tm=OUT_ROWS):
    m, d = x.shape
    tm = min(tm, seq)
    n_tiles = m // tm
    n_mem = kv.shape[1]
    z_blk0 = (CONV_CH + MEM_WIDTH) // PROJ_TILE
    once = pl.Buffered(1)

    def gated(i):
        return jnp.minimum(i, n_tiles - 1)

    def normed(i):
        return jnp.maximum(i - 1, 0)

    def h_cols(blk):
        return pl.BlockSpec((tm, PROJ_TILE), lambda i: (gated(i), blk))

    return pl.pallas_call(
        _out_kernel,
        grid=(n_tiles + 1,),
        in_specs=[pl.BlockSpec((tm, TOK_WIDTH), lambda i: (gated(i), 0)),
                  h_cols(CONV_CH // PROJ_TILE),
                  h_cols(z_blk0), h_cols(z_blk0 + 1), h_cols(z_blk0 + 2), h_cols(z_blk0 + 3),
                  pl.BlockSpec((1, n_mem, 2 * MEM_WIDTH), lambda i: (gated(i) * tm // seq, 0, 0)),
                  pl.BlockSpec((1, MIX_WIDTH, d), lambda i: (layer, 0, 0), pipeline_mode=once),
                  pl.BlockSpec((tm, d), lambda i: (normed(i), 0)),
                  pl.BlockSpec((1, d), lambda i: (0, 0)),
                  pl.BlockSpec((1, d), lambda i: (0, 0))],
        out_specs=[pl.BlockSpec((tm, d), lambda i: (normed(i), 0))] * 2,
        out_shape=[jax.ShapeDtypeStruct((m, d), _F32), jax.ShapeDtypeStruct((m, d), _BF16)],
        scratch_shapes=[pltpu.VMEM((tm, d), _F32)],
        compiler_params=_cparams(("arbitrary",)),
        name="gate_out_layernorm",
    )(tok, h, h, h, h, h, kv, w_out_b, x, ln_g.reshape(1, d), ln_b.reshape(1, d))


def kernel(x, mem, positions, w_in_ret, ret_norm_g, w_in_gdn, conv_w, a_log, dt_bias, gdn_norm_g,
           w_mem_kv, w_out, ln_g, ln_b):
    b, s, d = x.shape
    m = b * s
    n_mem = mem.shape[1]
    cos, sin = _rope_tables(positions)
    mem_b = mem.reshape(b * n_mem, d).astype(_BF16)
    w_out_b = w_out.astype(_BF16)
    xf = x.reshape(m, d)
    xb = xf.astype(_BF16)
    for i in range(DEPTH):
        j = i // 2
        if i % 2 == 0:
            h = _project(xb, w_in_ret, j, RET_COLS)
            tok = _retention(h.reshape(b, s, RET_COLS), cos, sin, ret_norm_g[j])
        else:
            h = _project(xb, w_in_gdn, j, RET_COLS)
            gc, beta = _gdn_gates(xb.reshape(b, s, d), w_in_gdn[j, :, RET_COLS:], a_log[j], dt_bias[j])
            tok = _gdn(h.reshape(b, s, RET_COLS), conv_w[j], gc, beta, gdn_norm_g[j])
        kv = _project(mem_b, w_mem_kv, i, 2 * MEM_WIDTH).reshape(b, n_mem, 2 * MEM_WIDTH)
        xf, xb = _gate_out_ln(tok.reshape(m, TOK_WIDTH), h, kv, w_out_b, i, xf, ln_g[i], ln_b[i], s)
    return xf.reshape(b, s, d)
```

```python
import functools

import numpy as np
import jax
import jax.numpy as jnp
from jax import lax
from jax.experimental import pallas as pl
from jax.experimental.pallas import tpu as pltpu

QK_HEAD_DIM = 128
N_QK_HEADS = 12
QK_WIDTH = N_QK_HEADS * QK_HEAD_DIM
TOK_WIDTH = 3072
RET_V_HEAD_DIM = 256
GDN_V_HEAD_DIM = 128
N_GDN_V_HEADS = 24
MEM_HEADS = 4
MEM_HEAD_DIM = 256
MEM_WIDTH = MEM_HEADS * MEM_HEAD_DIM
MIX_WIDTH = TOK_WIDTH + MEM_WIDTH
CONV_WIDTH = 4
CONV_CH = 2 * QK_WIDTH + TOK_WIDTH
RET_COLS = CONV_CH + MEM_WIDTH + MIX_WIDTH
ROPE_BASE = 10000.0
DEPTH = 4
DEEPNORM_ALPHA = (2.0 * DEPTH) ** 0.25
LN_EPS = 1e-5
NORM_EPS = 1e-6

V7X_VMEM_BYTES = 64 * 1024 * 1024
VMEM_LIMIT = V7X_VMEM_BYTES - 8 * 1024 * 1024
LANES = 128

PROJ_TILE = 1024
RET_CHUNK = 512
RET_HEADS_PER_STEP = 2
GDN_CHUNK = 128
GDN_ROWS = 512
CONV_HALO = 8
OUT_ROWS = 256
OUT_K_SPLIT = 4

_F32 = jnp.float32
_BF16 = jnp.bfloat16


def _cparams(sem):
    return pltpu.CompilerParams(dimension_semantics=sem, vmem_limit_bytes=VMEM_LIMIT)


def _dot(a, b):
    return jnp.dot(a, b, preferred_element_type=_F32)


def _bdot(a, b):
    return _dot(a.astype(_BF16), b.astype(_BF16))


def _dot_nt(a, b):
    return lax.dot_general(a, b, (((1,), (1,)), ((), ())), preferred_element_type=_F32)


def _dot_tn(a, b):
    return lax.dot_general(a, b, (((0,), (0,)), ((), ())), preferred_element_type=_F32)


def _silu(z):
    return z * jax.nn.sigmoid(z)


def _interleave(*stages):
    live = [[gen, count, 0] for gen, count in stages]
    while live:
        stage = min(live, key=lambda st: (st[2] + 1) / st[1])
        try:
            next(stage[0])
            stage[2] += 1
        except StopIteration:
            live.remove(stage)


def _proj_kernel(w_is_transposed, x_ref, w_ref, o_ref, wb_ref):
    @pl.when(pl.program_id(1) == 0)
    def _():
        w = w_ref[0]
        wb_ref[...] = (w.T if w_is_transposed else w).astype(_BF16)

    o_ref[...] = _dot(x_ref[...], wb_ref[...]).astype(o_ref.dtype)


def _project(x, w_stack, layer, n_cols, w_is_transposed=False, tile=PROJ_TILE):
    m, k = x.shape
    tm = min(tile, m)
    assert m % tm == 0 and n_cols % tile == 0
    if w_is_transposed:
        w_spec = pl.BlockSpec((1, tile, k), lambda j, i: (layer, j, 0))
    else:
        w_spec = pl.BlockSpec((1, k, tile), lambda j, i: (layer, 0, j))
    return pl.pallas_call(
        functools.partial(_proj_kernel, w_is_transposed),
        grid=(n_cols // tile, m // tm),
        in_specs=[pl.BlockSpec((tm, k), lambda j, i: (i, 0)), w_spec],
        out_specs=pl.BlockSpec((tm, tile), lambda j, i: (i, j)),
        out_shape=jax.ShapeDtypeStruct((m, n_cols), _BF16),
        scratch_shapes=[pltpu.VMEM((k, tile), _BF16)],
        compiler_params=_cparams(("parallel", "arbitrary")),
        name="proj_matmul",
    )(x, w_stack)


def _rope_kernel(pos_ref, freq_ref, cos_ref, sin_ref):
    ang = pos_ref[0].astype(_F32) * freq_ref[...]
    lane = lax.broadcasted_iota(jnp.int32, ang.shape, 1)
    s = jnp.sin(ang)
    cos_ref[0] = jnp.cos(ang)
    sin_ref[0] = jnp.where(lane < QK_HEAD_DIM // 2, -s, s)


def _rope_tables(positions, ts=512):
    b, s = positions.shape
    half = QK_HEAD_DIM // 2
    inv_freq = (ROPE_BASE ** (-np.arange(half, dtype=np.float32) / np.float32(half))).astype(np.float32)
    freq = jnp.asarray(np.concatenate([inv_freq, inv_freq])[None, :])
    ts = min(ts, s)
    out = jax.ShapeDtypeStruct((b, s, QK_HEAD_DIM), _F32)
    return pl.pallas_call(
        _rope_kernel,
        grid=(b, s // ts),
        in_specs=[pl.BlockSpec((1, ts, 1), lambda i, j: (i, j, 0)),
                  pl.BlockSpec((1, QK_HEAD_DIM), lambda i, j: (0, 0))],
        out_specs=[pl.BlockSpec((1, ts, QK_HEAD_DIM), lambda i, j: (i, j, 0))] * 2,
        out_shape=[out, out],
        compiler_params=_cparams(("parallel", "parallel")),
        name="rope_tables",
    )(positions.reshape(b, s, 1), freq)


def _retention_kernel(lg_ref, q_ref, k_ref, v_ref, cos_ref, sin_ref, g_ref, o_ref, state_ref, decay_ref):
    c = q_ref.shape[1]
    dk, dv = QK_HEAD_DIM, RET_V_HEAD_DIM
    pair = pl.program_id(1)

    @pl.when(pl.program_id(2) == 0)
    def _():
        state_ref[...] = jnp.zeros_like(state_ref)
        ii = lax.broadcasted_iota(jnp.int32, (c, c), 0)
        jj = lax.broadcasted_iota(jnp.int32, (c, c), 1)
        rel = (ii - jj).astype(_F32)
        for j in range(RET_HEADS_PER_STEP):
            lg = lg_ref[RET_HEADS_PER_STEP * pair + j]
            decay_ref[j] = jnp.where(rel >= 0, jnp.exp(lg * jnp.maximum(rel, 0.0)), 0.0)

    cos = cos_ref[0]
    sin = sin_ref[0]
    idx = lax.broadcasted_iota(jnp.int32, (c, dk), 0).astype(_F32)

    def head(j):
        lg = lg_ref[RET_HEADS_PER_STEP * pair + j]
        q = q_ref[0, :, j * dk:(j + 1) * dk].astype(_F32)
        k = k_ref[0, :, j * dk:(j + 1) * dk].astype(_F32)
        qr = q * cos + pltpu.roll(q, dk // 2, 1) * sin
        kr = (k * cos + pltpu.roll(k, dk // 2, 1) * sin) * (dk ** -0.5)
        qb = qr.astype(_BF16)
        kb = kr.astype(_BF16)
        q_in = (qr * jnp.exp((idx + 1.0) * lg)).astype(_BF16)
        k_out = (kr * jnp.exp((c - 1.0 - idx) * lg)).astype(_BF16)
        yield
        scores = (_dot_nt(qb, kb) * decay_ref[j]).astype(_BF16)
        yield
        vb = v_ref[0, :, j * dv:(j + 1) * dv]
        state = state_ref[j]
        o = _dot(jnp.concatenate([scores, q_in], 1), jnp.concatenate([vb, state.astype(_BF16)], 0))
        state_ref[j] = state * jnp.exp(c * lg) + _dot_tn(k_out, vb)
        yield
        mu = jnp.mean(o, -1, keepdims=True)
        d = o - mu
        var = jnp.mean(d * d, -1, keepdims=True)
        o_ref[0, :, j * dv:(j + 1) * dv] = (
            d * lax.rsqrt(var + NORM_EPS) * g_ref[:, j * dv:(j + 1) * dv]).astype(o_ref.dtype)

    _interleave(*[(head(j), 4) for j in range(RET_HEADS_PER_STEP)])


def _retention(h, cos, sin, norm_g, c=RET_CHUNK):
    b, s, _ = h.shape
    c = min(c, s)
    hp = RET_HEADS_PER_STEP
    log_gamma = jnp.asarray(np.log1p(-np.exp2(-5.0 - np.arange(N_QK_HEADS, dtype=np.float64))).astype(np.float32))
    n_pairs = N_QK_HEADS // hp
    v_blk0 = 2 * QK_WIDTH // (hp * RET_V_HEAD_DIM)
    grid_spec = pltpu.PrefetchScalarGridSpec(
        num_scalar_prefetch=1,
        grid=(b, n_pairs, s // c),
        in_specs=[
            pl.BlockSpec((1, c, hp * QK_HEAD_DIM), lambda i, hd, t, lg: (i, t, hd)),
            pl.BlockSpec((1, c, hp * QK_HEAD_DIM), lambda i, hd, t, lg: (i, t, n_pairs + hd)),
            pl.BlockSpec((1, c, hp * RET_V_HEAD_DIM), lambda i, hd, t, lg: (i, t, v_blk0 + hd)),
            pl.BlockSpec((1, c, QK_HEAD_DIM), lambda i, hd, t, lg: (i, t, 0)),
            pl.BlockSpec((1, c, QK_HEAD_DIM), lambda i, hd, t, lg: (i, t, 0)),
            pl.BlockSpec((1, hp * RET_V_HEAD_DIM), lambda i, hd, t, lg: (0, hd)),
        ],
        out_specs=pl.BlockSpec((1, c, hp * RET_V_HEAD_DIM), lambda i, hd, t, lg: (i, t, hd)),
        scratch_shapes=[pltpu.VMEM((hp, QK_HEAD_DIM, RET_V_HEAD_DIM), _F32),
                        pltpu.VMEM((hp, c, c), _F32)],
    )
    return pl.pallas_call(
        _retention_kernel,
        grid_spec=grid_spec,
        out_shape=jax.ShapeDtypeStruct((b, s, TOK_WIDTH), _BF16),
        compiler_params=_cparams(("parallel", "parallel", "arbitrary")),
        name="retention",
    )(log_gamma, h, h, h, cos, sin, norm_g.reshape(1, TOK_WIDTH))


def _gdn_gate_kernel(x_ref, w_ref, alog_ref, dtb_ref, gc_ref, beta_ref):
    ab = _dot_nt(w_ref[...].T.astype(_BF16), x_ref[0])
    a = ab[:N_GDN_V_HEADS]
    bb = ab[N_GDN_V_HEADS:2 * N_GDN_V_HEADS]
    g = -jnp.exp(alog_ref[...]) * jax.nn.softplus(a + dtb_ref[...])
    lane = lax.broadcasted_iota(jnp.int32, g.shape, 1) % GDN_CHUNK
    shift = 1
    while shift < GDN_CHUNK:
        g = g + jnp.where(lane >= shift, pltpu.roll(g, shift, 1), 0.0)
        shift *= 2
    gc_ref[0] = g
    beta_ref[0] = jax.nn.sigmoid(bb)


def _gdn_gates(xb, w_ab, a_log, dt_bias, ts=512):
    b, s, d = xb.shape
    ts = min(ts, s)
    w_ab = jnp.pad(w_ab, ((0, 0), (0, LANES - w_ab.shape[1])))
    out = jax.ShapeDtypeStruct((b, N_GDN_V_HEADS, s), _F32)
    return pl.pallas_call(
        _gdn_gate_kernel,
        grid=(b, s // ts),
        in_specs=[pl.BlockSpec((1, ts, d), lambda i, j: (i, j, 0)),
                  pl.BlockSpec((d, LANES), lambda i, j: (0, 0)),
                  pl.BlockSpec((N_GDN_V_HEADS, 1), lambda i, j: (0, 0)),
                  pl.BlockSpec((N_GDN_V_HEADS, 1), lambda i, j: (0, 0))],
        out_specs=[pl.BlockSpec((1, N_GDN_V_HEADS, ts), lambda i, j: (i, 0, j))] * 2,
        out_shape=[out, out],
        compiler_params=_cparams(("parallel", "parallel")),
        name="gdn_gates",
    )(xb, w_ab, a_log.reshape(-1, 1).astype(_F32), dt_bias.reshape(-1, 1).astype(_F32))


def _unit_lower_inverses(lows, ii, jj):
    n = lows[0].shape[0]
    xor = ii ^ jj
    eye = (ii == jj).astype(_F32)
    d1s = [jnp.where(xor < 8, low, 0.0) for low in lows]
    d1bs = [d1.astype(_BF16) for d1 in d1s]
    d2bs = [_dot(d1b, d1b).astype(_BF16) for d1b in d1bs]
    yield None
    ts = [eye - d1 for d1 in d1s]
    ts = [t + _dot(t.astype(_BF16), d2b) for t, d2b in zip(ts, d2bs)]
    yield None
    d4bs = [_dot(d2b, d2b).astype(_BF16) for d2b in d2bs]
    yield None
    ts = [t + _dot(t.astype(_BF16), d4b) for t, d4b in zip(ts, d4bs)]
    yield None
    neg_lows = [(-low).astype(_BF16) for low in lows]
    size = 8
    while size < n:
        band = (xor >= size) & (xor < 2 * size)
        tbs = [t.astype(_BF16) for t in ts]
        ys = [_dot(tb, nl).astype(_BF16) for tb, nl in zip(tbs, neg_lows)]
        yield None
        ts = [jnp.where(band, _dot(y, tb), t) for t, y, tb in zip(ts, ys, tbs)]
        yield None
        size *= 2
    yield ts


def _conv_silu_rows(buf_ref, w, r0, n):
    y = None
    for lag in range(CONV_WIDTH):
        lo = CONV_HALO + r0 - lag
        term = buf_ref[lo:lo + n, :] * w[CONV_WIDTH - 1 - lag:CONV_WIDTH - lag, :]
        y = term if y is None else y + term
    return _silu(y)


def _gdn_kernel(blocks_per_seq, q_ref, k_ref, v_ref, wq_ref, wk_ref, wv_ref, gc_ref, beta_ref, ng_ref, o_ref,
                qbuf, kbuf, vbuf, state_ref, low_s, t_s, kp_s, qg_s, rhs_s, gl_s):
    rows = q_ref.shape[1]
    n = GDN_CHUNK
    dv = GDN_V_HEAD_DIM
    n_sys = 2 * (rows // n)
    g_step = pl.program_id(0)
    slot = g_step % 2

    @pl.when(g_step == 0)
    def _():
        for ref in (low_s, t_s, kp_s, qg_s, rhs_s, gl_s, state_ref):
            ref[...] = jnp.zeros_like(ref)

    @pl.when(g_step % blocks_per_seq == 0)
    def _():
        for ref in (qbuf, kbuf, vbuf):
            ref[0:CONV_HALO, :] = jnp.zeros((CONV_HALO, ref.shape[1]), _F32)

    ii = lax.broadcasted_iota(jnp.int32, (n, n), 0)
    jj = lax.broadcasted_iota(jnp.int32, (n, n), 1)
    causal = ii >= jj
    strict = ii > jj

    def recurrence_stage():
        fresh = (g_step - 2) % blocks_per_seq == 0
        ng = ng_ref[...]
        prods = []
        for idx in range(n_sys):
            wu = _dot(t_s[idx], rhs_s[slot, idx]).astype(_BF16)
            both = _dot(kp_s[slot, idx], wu)
            q_eff = (qg_s[slot, idx] - both[n:, :n]).astype(_BF16)
            prods.append((jnp.concatenate([q_eff, both[:n, :n].astype(_BF16)], 0), both[:n, n:],
                          both[n:, n:], gl_s[slot, idx, 0:1, :]))
            yield
        states = [jnp.where(fresh, 0.0, state_ref[e]) for e in range(2)]
        for idx, (qa, n_mat, pu, gl) in enumerate(prods):
            c, e = divmod(idx, 2)
            qa_s = _dot(qa, states[e].astype(_BF16))
            out = qa_s[:n] + pu
            states[e] = states[e] * gl - qa_s[n:] + n_mat
            out = out * lax.rsqrt(jnp.mean(out * out, -1, keepdims=True) + NORM_EPS) * ng
            o_ref[0, c * n:(c + 1) * n, e * dv:(e + 1) * dv] = out.astype(o_ref.dtype)
            yield
        state_ref[0] = states[0]
        state_ref[1] = states[1]

    def inverse_stage():
        for t_invs in _unit_lower_inverses([low_s[idx] for idx in range(n_sys)], ii, jj):
            yield
        for idx, t_inv in enumerate(t_invs):
            t_s[idx] = t_inv.astype(_BF16)

    def operand_stage():
        bufs = ((qbuf, q_ref), (kbuf, k_ref), (vbuf, v_ref))
        for buf, ref in bufs:
            buf[CONV_HALO:CONV_HALO + rows, :] = ref[0].astype(_F32)
        yield
        wq, wk, wv = wq_ref[...], wk_ref[...], wv_ref[...]
        for c in range(rows // n):
            r0 = c * n
            qc = _conv_silu_rows(qbuf, wq, r0, n)
            kc = _conv_silu_rows(kbuf, wk, r0, n)
            qn = qc * (lax.rsqrt(jnp.sum(qc * qc, -1, keepdims=True) + NORM_EPS) * (QK_HEAD_DIM ** -0.5))
            kn = kc * lax.rsqrt(jnp.sum(kc * kc, -1, keepdims=True) + NORM_EPS)
            kb = kn.astype(_BF16)
            kq = _dot_nt(jnp.concatenate([kb, qn.astype(_BF16)], 0), kb)
            kk, qk = kq[:n], kq[n:]
            yield
            vc = _conv_silu_rows(vbuf, wv, r0, n)
            for e in range(2):
                idx = 2 * c + e
                g_row = jnp.broadcast_to(gc_ref[0, 0, e:e + 1, r0:r0 + n], (n, n))
                g_col = g_row.T
                b_col = jnp.broadcast_to(beta_ref[0, 0, e:e + 1, r0:r0 + n], (n, n)).T
                decay = jnp.where(causal, jnp.exp(jnp.where(causal, g_col - g_row, 0.0)), 0.0)
                e_g = jnp.exp(g_col)
                g_last = g_col[n - 1:n, :]
                v_e = vc[:, e * dv:(e + 1) * dv]
                low_s[idx] = jnp.where(strict, b_col * kk * decay, 0.0)
                k_tail_t = (kn * jnp.exp(g_last - g_col)).T
                kp_s[slot, idx] = jnp.concatenate([k_tail_t, qk * decay], 0).astype(_BF16)
                qg_s[slot, idx] = qn * e_g
                rhs_s[slot, idx] = jnp.concatenate([b_col * e_g * kn, b_col * v_e], 1).astype(_BF16)
                gl_s[slot, idx] = jnp.broadcast_to(jnp.exp(g_last), (CONV_HALO, dv))
                yield
        for buf, _ in bufs:
            buf[0:CONV_HALO, :] = buf[rows:rows + CONV_HALO, :]

    n_chunks = rows // n
    _interleave((recurrence_stage(), 2 * n_sys + 1), (inverse_stage(), 13), (operand_stage(), 3 * n_chunks + 2))


def _gdn(h, conv_w, gc, beta, norm_g, rows=GDN_ROWS):
    b, s, _ = h.shape
    rows = min(rows, s)
    nb = s // rows
    total = b * N_QK_HEADS * nb
    v_blk0 = 2 * QK_WIDTH // (2 * GDN_V_HEAD_DIM)
    gc = gc.reshape(b, N_QK_HEADS, 2, s)
    beta = beta.reshape(b, N_QK_HEADS, 2, s)
    pair = 2 * GDN_V_HEAD_DIM
    n_sys = 2 * (rows // GDN_CHUNK)

    def where(g):
        g = jnp.minimum(g, total - 1)
        return g // (N_QK_HEADS * nb), (g // nb) % N_QK_HEADS, g % nb

    def cur(col0):
        def index(g):
            i, hd, t = where(g)
            return i, t, col0 + hd
        return index

    def gate_index(g):
        i, hd, t = where(g)
        return i, hd, 0, t

    def out_index(g):
        i, hd, t = where(jnp.maximum(g - 2, 0))
        return i, t, hd

    return pl.pallas_call(
        functools.partial(_gdn_kernel, nb),
        grid=(total + 2,),
        in_specs=[
            pl.BlockSpec((1, rows, QK_HEAD_DIM), cur(0)),
            pl.BlockSpec((1, rows, QK_HEAD_DIM), cur(N_QK_HEADS)),
            pl.BlockSpec((1, rows, pair), cur(v_blk0)),
            pl.BlockSpec((CONV_WIDTH, QK_HEAD_DIM), lambda g: (0, where(g)[1])),
            pl.BlockSpec((CONV_WIDTH, QK_HEAD_DIM), lambda g: (0, N_QK_HEADS + where(g)[1])),
            pl.BlockSpec((CONV_WIDTH, pair), lambda g: (0, v_blk0 + where(g)[1])),
            pl.BlockSpec((1, 1, 2, rows), gate_index),
            pl.BlockSpec((1, 1, 2, rows), gate_index),
            pl.BlockSpec((1, GDN_V_HEAD_DIM), lambda g: (0, 0)),
        ],
        out_specs=pl.BlockSpec((1, rows, pair), out_index),
        out_shape=jax.ShapeDtypeStruct((b, s, TOK_WIDTH), _BF16),
        scratch_shapes=[pltpu.VMEM((rows + CONV_HALO, QK_HEAD_DIM), _F32),
                        pltpu.VMEM((rows + CONV_HALO, QK_HEAD_DIM), _F32),
                        pltpu.VMEM((rows + CONV_HALO, pair), _F32),
                        pltpu.VMEM((2, QK_HEAD_DIM, GDN_V_HEAD_DIM), _F32),
                        pltpu.VMEM((n_sys, GDN_CHUNK, GDN_CHUNK), _F32),
                        pltpu.VMEM((n_sys, GDN_CHUNK, GDN_CHUNK), _BF16),
                        pltpu.VMEM((2, n_sys, QK_HEAD_DIM + GDN_CHUNK, GDN_CHUNK), _BF16),
                        pltpu.VMEM((2, n_sys, GDN_CHUNK, QK_HEAD_DIM), _F32),
                        pltpu.VMEM((2, n_sys, GDN_CHUNK, QK_HEAD_DIM + GDN_V_HEAD_DIM), _BF16),
                        pltpu.VMEM((2, n_sys, CONV_HALO, GDN_V_HEAD_DIM), _F32)],
        compiler_params=_cparams(("arbitrary",)),
        name="gated_delta",
    )(h, h, h, conv_w, conv_w, conv_w, gc, beta, norm_g.reshape(1, GDN_V_HEAD_DIM))


def _out_kernel(tok_ref, mq_ref, z0_ref, z1_ref, z2_ref, z3_ref, kv_ref, w_ref, x_ref, g_ref, b_ref,
                o_ref, ob_ref, y_s):
    tm, d = x_ref.shape
    kc = MIX_WIDTH // OUT_K_SPLIT
    z_refs = (z0_ref, z1_ref, z2_ref, z3_ref)

    @pl.when(pl.program_id(0) == 0)
    def _():
        y_s[...] = jnp.zeros_like(y_s)

    def attend(m):
        lo = m * MEM_HEAD_DIM
        mk = kv_ref[0, :, lo:lo + MEM_HEAD_DIM]
        mv = kv_ref[0, :, MEM_WIDTH + lo:MEM_WIDTH + lo + MEM_HEAD_DIM]
        sc = _dot_nt(mq_ref[:, lo:lo + MEM_HEAD_DIM], mk) * (MEM_HEAD_DIM ** -0.5)
        p = jnp.exp(sc - jnp.max(sc, -1, keepdims=True))
        p = p / jnp.sum(p, -1, keepdims=True)
        return _dot(p.astype(_BF16), mv)

    def branch_chunk(k):
        lo = k * kc
        z = z_refs[lo // PROJ_TILE][:, lo % PROJ_TILE:lo % PROJ_TILE + kc].astype(_F32)
        if lo < TOK_WIDTH:
            val = tok_ref[:, lo:lo + kc].astype(_F32)
        else:
            first = (lo - TOK_WIDTH) // MEM_HEAD_DIM
            val = jnp.concatenate([attend(first + j) for j in range(kc // MEM_HEAD_DIM)], 1)
        return (val * _silu(z)).astype(_BF16)

    def project_stage():
        acc = None
        for k in range(OUT_K_SPLIT):
            part = _dot(branch_chunk(k), w_ref[0, k * kc:(k + 1) * kc, :])
            acc = part if acc is None else acc + part
            yield
        y_s[...] = acc

    def norm_stage():
        rows = tm // OUT_K_SPLIT
        for blk in range(OUT_K_SPLIT):
            sl = slice(blk * rows, (blk + 1) * rows)
            r = DEEPNORM_ALPHA * x_ref[sl, :] + y_s[sl, :]
            mu = jnp.mean(r, -1, keepdims=True)
            dev = r - mu
            var = jnp.mean(dev * dev, -1, keepdims=True)
            out = dev * lax.rsqrt(var + LN_EPS) * g_ref[...] + b_ref[...]
            o_ref[sl, :] = out
            ob_ref[sl, :] = out.astype(_BF16)
            yield

    _interleave((norm_stage(), OUT_K_SPLIT), (project_stage(), OUT_K_SPLIT + 1))


def _gate_out_ln(tok, h, kv, w_out_b, layer, x, ln_g, ln_b, seq, tm=OUT_ROWS):
    m, d = x.shape
    tm = min(tm, seq)
    n_tiles = m // tm
    n_mem = kv.shape[1]
    z_blk0 = (CONV_CH + MEM_WIDTH) // PROJ_TILE
    once = pl.Buffered(1)

    def gated(i):
        return jnp.minimum(i, n_tiles - 1)

    def normed(i):
        return jnp.maximum(i - 1, 0)

    def h_cols(blk):
        return pl.BlockSpec((tm, PROJ_TILE), lambda i: (gated(i), blk))

    return pl.pallas_call(
        _out_kernel,
        grid=(n_tiles + 1,),
        in_specs=[pl.BlockSpec((tm, TOK_WIDTH), lambda i: (gated(i), 0)),
                  h_cols(CONV_CH // PROJ_TILE),
                  h_cols(z_blk0), h_cols(z_blk0 + 1), h_cols(z_blk0 + 2), h_cols(z_blk0 + 3),
                  pl.BlockSpec((1, n_mem, 2 * MEM_WIDTH), lambda i: (gated(i) * tm // seq, 0, 0)),
                  pl.BlockSpec((1, MIX_WIDTH, d), lambda i: (layer, 0, 0), pipeline_mode=once),
                  pl.BlockSpec((tm, d), lambda i: (normed(i), 0)),
                  pl.BlockSpec((1, d), lambda i: (0, 0)),
                  pl.BlockSpec((1, d), lambda i: (0, 0))],
        out_specs=[pl.BlockSpec((tm, d), lambda i: (normed(i), 0))] * 2,
        out_shape=[jax.ShapeDtypeStruct((m, d), _F32), jax.ShapeDtypeStruct((m, d), _BF16)],
        scratch_shapes=[pltpu.VMEM((tm, d), _F32)],
        compiler_params=_cparams(("arbitrary",)),
        name="gate_out_layernorm",
    )(tok, h, h, h, h, h, kv, w_out_b, x, ln_g.reshape(1, d), ln_b.reshape(1, d))


def kernel(x, mem, positions, w_in_ret, ret_norm_g, w_in_gdn, conv_w, a_log, dt_bias, gdn_norm_g,
           w_mem_kv, w_out, ln_g, ln_b):
    b, s, d = x.shape
    m = b * s
    n_mem = mem.shape[1]
    cos, sin = _rope_tables(positions)
    mem_b = mem.reshape(b * n_mem, d).astype(_BF16)
    w_out_b = w_out.astype(_BF16)
    w_in_gdn_t = jnp.swapaxes(w_in_gdn, 1, 2)
    xf = x.reshape(m, d)
    xb = xf.astype(_BF16)
    for i in range(DEPTH):
        j = i // 2
        if i % 2 == 0:
            h = _project(xb, w_in_ret, j, RET_COLS)
            tok = _retention(h.reshape(b, s, RET_COLS), cos, sin, ret_norm_g[j])
        else:
            h = _project(xb, w_in_gdn_t, j, RET_COLS, w_is_transposed=True)
            gc, beta = _gdn_gates(xb.reshape(b, s, d), w_in_gdn[j, :, RET_COLS:], a_log[j], dt_bias[j])
            tok = _gdn(h.reshape(b, s, RET_COLS), conv_w[j], gc, beta, gdn_norm_g[j])
        kv = _project(mem_b, w_mem_kv, i, 2 * MEM_WIDTH).reshape(b, n_mem, 2 * MEM_WIDTH)
        xf, xb = _gate_out_ln(tok.reshape(m, TOK_WIDTH), h, kv, w_out_b, i, xf, ln_g[i], ln_b[i], s)
    return xf.reshape(b, s, d)
```

```python
import functools

import numpy as np
import jax
import jax.numpy as jnp
from jax import lax
from jax.experimental import pallas as pl
from jax.experimental.pallas import tpu as pltpu

QK_HEAD_DIM = 128
N_QK_HEADS = 12
QK_WIDTH = N_QK_HEADS * QK_HEAD_DIM
TOK_WIDTH = 3072
RET_V_HEAD_DIM = 256
GDN_V_HEAD_DIM = 128
N_GDN_V_HEADS = 24
MEM_HEADS = 4
MEM_HEAD_DIM = 256
MEM_WIDTH = MEM_HEADS * MEM_HEAD_DIM
MIX_WIDTH = TOK_WIDTH + MEM_WIDTH
CONV_WIDTH = 4
CONV_CH = 2 * QK_WIDTH + TOK_WIDTH
RET_COLS = CONV_CH + MEM_WIDTH + MIX_WIDTH
ROPE_BASE = 10000.0
DEPTH = 4
DEEPNORM_ALPHA = (2.0 * DEPTH) ** 0.25
LN_EPS = 1e-5
NORM_EPS = 1e-6

V7X_VMEM_BYTES = 64 * 1024 * 1024
VMEM_LIMIT = V7X_VMEM_BYTES - 8 * 1024 * 1024
LANES = 128

PROJ_TILE = 1024
PROJ_ROWS = 2048
RET_CHUNK = 512
RET_HEADS_PER_STEP = 4
GDN_CHUNK = 128
GDN_ROWS = 512
CONV_HALO = 8
OUT_ROWS = 256
OUT_K_SPLIT = 4

_F32 = jnp.float32
_BF16 = jnp.bfloat16


def _cparams(sem):
    return pltpu.CompilerParams(dimension_semantics=sem, vmem_limit_bytes=VMEM_LIMIT)


def _dot(a, b):
    return jnp.dot(a, b, preferred_element_type=_F32)


def _bdot(a, b):
    return _dot(a.astype(_BF16), b.astype(_BF16))


def _dot_nt(a, b):
    return lax.dot_general(a, b, (((1,), (1,)), ((), ())), preferred_element_type=_F32)


def _dot_tn(a, b):
    return lax.dot_general(a, b, (((0,), (0,)), ((), ())), preferred_element_type=_F32)


def _silu(z):
    return z * jax.nn.sigmoid(z)


def _interleave(*stages):
    live = [[gen, count, 0] for gen, count in stages]
    while live:
        stage = min(live, key=lambda st: (st[2] + 1) / st[1])
        try:
            next(stage[0])
            stage[2] += 1
        except StopIteration:
            live.remove(stage)


def _proj_kernel(w_is_transposed, x_ref, w_ref, o_ref, wb_ref):
    @pl.when(pl.program_id(1) == 0)
    def _():
        w = w_ref[0]
        wb_ref[...] = (w.T if w_is_transposed else w).astype(_BF16)

    o_ref[...] = _dot(x_ref[...], wb_ref[...]).astype(o_ref.dtype)


def _project(x, w_stack, layer, n_cols, w_is_transposed=False, tile=PROJ_TILE, rows=PROJ_ROWS):
    m, k = x.shape
    tm = min(rows, m)
    assert m % tm == 0 and n_cols % tile == 0
    if w_is_transposed:
        w_spec = pl.BlockSpec((1, tile, k), lambda j, i: (layer, j, 0))
    else:
        w_spec = pl.BlockSpec((1, k, tile), lambda j, i: (layer, 0, j))
    return pl.pallas_call(
        functools.partial(_proj_kernel, w_is_transposed),
        grid=(n_cols // tile, m // tm),
        in_specs=[pl.BlockSpec((tm, k), lambda j, i: (i, 0)), w_spec],
        out_specs=pl.BlockSpec((tm, tile), lambda j, i: (i, j)),
        out_shape=jax.ShapeDtypeStruct((m, n_cols), _BF16),
        scratch_shapes=[pltpu.VMEM((k, tile), _BF16)],
        compiler_params=_cparams(("parallel", "arbitrary")),
        name="proj_matmul",
    )(x, w_stack)


def _rope_kernel(pos_ref, freq_ref, cos_ref, sin_ref):
    ang = pos_ref[0].astype(_F32) * freq_ref[...]
    lane = lax.broadcasted_iota(jnp.int32, ang.shape, 1)
    s = jnp.sin(ang)
    cos_ref[0] = jnp.cos(ang)
    sin_ref[0] = jnp.where(lane < QK_HEAD_DIM // 2, -s, s)


def _rope_tables(positions, ts=512):
    b, s = positions.shape
    half = QK_HEAD_DIM // 2
    inv_freq = (ROPE_BASE ** (-np.arange(half, dtype=np.float32) / np.float32(half))).astype(np.float32)
    freq = jnp.asarray(np.concatenate([inv_freq, inv_freq])[None, :])
    ts = min(ts, s)
    out = jax.ShapeDtypeStruct((b, s, QK_HEAD_DIM), _F32)
    return pl.pallas_call(
        _rope_kernel,
        grid=(b, s // ts),
        in_specs=[pl.BlockSpec((1, ts, 1), lambda i, j: (i, j, 0)),
                  pl.BlockSpec((1, QK_HEAD_DIM), lambda i, j: (0, 0))],
        out_specs=[pl.BlockSpec((1, ts, QK_HEAD_DIM), lambda i, j: (i, j, 0))] * 2,
        out_shape=[out, out],
        compiler_params=_cparams(("parallel", "parallel")),
        name="rope_tables",
    )(positions.reshape(b, s, 1), freq)


def _retention_kernel(lg_ref, q_ref, k_ref, v_ref, cos_ref, sin_ref, g_ref, o_ref, state_ref, decay_ref):
    c = q_ref.shape[1]
    dk, dv = QK_HEAD_DIM, RET_V_HEAD_DIM
    pair = pl.program_id(1)

    @pl.when(pl.program_id(2) == 0)
    def _():
        state_ref[...] = jnp.zeros_like(state_ref)
        ii = lax.broadcasted_iota(jnp.int32, (c, c), 0)
        jj = lax.broadcasted_iota(jnp.int32, (c, c), 1)
        rel = (ii - jj).astype(_F32)
        for j in range(RET_HEADS_PER_STEP):
            lg = lg_ref[RET_HEADS_PER_STEP * pair + j]
            decay_ref[j] = jnp.where(rel >= 0, jnp.exp(lg * jnp.maximum(rel, 0.0)), 0.0)

    cos = cos_ref[0]
    sin = sin_ref[0]
    idx = lax.broadcasted_iota(jnp.int32, (c, dk), 0).astype(_F32)

    def head(j):
        lg = lg_ref[RET_HEADS_PER_STEP * pair + j]
        q = q_ref[0, :, j * dk:(j + 1) * dk].astype(_F32)
        k = k_ref[0, :, j * dk:(j + 1) * dk].astype(_F32)
        qr = q * cos + pltpu.roll(q, dk // 2, 1) * sin
        kr = (k * cos + pltpu.roll(k, dk // 2, 1) * sin) * (dk ** -0.5)
        qb = qr.astype(_BF16)
        kb = kr.astype(_BF16)
        q_in = (qr * jnp.exp((idx + 1.0) * lg)).astype(_BF16)
        k_out = (kr * jnp.exp((c - 1.0 - idx) * lg)).astype(_BF16)
        yield
        scores = (_dot_nt(qb, kb) * decay_ref[j]).astype(_BF16)
        yield
        vb = v_ref[0, :, j * dv:(j + 1) * dv]
        state = state_ref[j]
        o = _dot(jnp.concatenate([scores, q_in], 1), jnp.concatenate([vb, state.astype(_BF16)], 0))
        state_ref[j] = state * jnp.exp(c * lg) + _dot_tn(k_out, vb)
        yield
        mu = jnp.mean(o, -1, keepdims=True)
        d = o - mu
        var = jnp.mean(d * d, -1, keepdims=True)
        o_ref[0, :, j * dv:(j + 1) * dv] = (
            d * lax.rsqrt(var + NORM_EPS) * g_ref[:, j * dv:(j + 1) * dv]).astype(o_ref.dtype)

    _interleave(*[(head(j), 4) for j in range(RET_HEADS_PER_STEP)])


def _retention(h, cos, sin, norm_g, c=RET_CHUNK):
    b, s, _ = h.shape
    c = min(c, s)
    hp = RET_HEADS_PER_STEP
    log_gamma = jnp.asarray(np.log1p(-np.exp2(-5.0 - np.arange(N_QK_HEADS, dtype=np.float64))).astype(np.float32))
    n_pairs = N_QK_HEADS // hp
    v_blk0 = 2 * QK_WIDTH // (hp * RET_V_HEAD_DIM)
    grid_spec = pltpu.PrefetchScalarGridSpec(
        num_scalar_prefetch=1,
        grid=(b, n_pairs, s // c),
        in_specs=[
            pl.BlockSpec((1, c, hp * QK_HEAD_DIM), lambda i, hd, t, lg: (i, t, hd)),
            pl.BlockSpec((1, c, hp * QK_HEAD_DIM), lambda i, hd, t, lg: (i, t, n_pairs + hd)),
            pl.BlockSpec((1, c, hp * RET_V_HEAD_DIM), lambda i, hd, t, lg: (i, t, v_blk0 + hd)),
            pl.BlockSpec((1, c, QK_HEAD_DIM), lambda i, hd, t, lg: (i, t, 0)),
            pl.BlockSpec((1, c, QK_HEAD_DIM), lambda i, hd, t, lg: (i, t, 0)),
            pl.BlockSpec((1, hp * RET_V_HEAD_DIM), lambda i, hd, t, lg: (0, hd)),
        ],
        out_specs=pl.BlockSpec((1, c, hp * RET_V_HEAD_DIM), lambda i, hd, t, lg: (i, t, hd)),
        scratch_shapes=[pltpu.VMEM((hp, QK_HEAD_DIM, RET_V_HEAD_DIM), _F32),
                        pltpu.VMEM((hp, c, c), _F32)],
    )
    return pl.pallas_call(
        _retention_kernel,
        grid_spec=grid_spec,
        out_shape=jax.ShapeDtypeStruct((b, s, TOK_WIDTH), _BF16),
        compiler_params=_cparams(("parallel", "parallel", "arbitrary")),
        name="retention",
    )(log_gamma, h, h, h, cos, sin, norm_g.reshape(1, TOK_WIDTH))


def _gdn_gate_kernel(x_ref, w_ref, alog_ref, dtb_ref, gc_ref, beta_ref):
    ab = _dot_nt(w_ref[...].T.astype(_BF16), x_ref[0])
    a = ab[:N_GDN_V_HEADS]
    bb = ab[N_GDN_V_HEADS:2 * N_GDN_V_HEADS]
    g = -jnp.exp(alog_ref[...]) * jax.nn.softplus(a + dtb_ref[...])
    lane = lax.broadcasted_iota(jnp.int32, g.shape, 1) % GDN_CHUNK
    shift = 1
    while shift < GDN_CHUNK:
        g = g + jnp.where(lane >= shift, pltpu.roll(g, shift, 1), 0.0)
        shift *= 2
    gc_ref[0] = g
    beta_ref[0] = jax.nn.sigmoid(bb)


def _gdn_gates(xb, w_ab, a_log, dt_bias, ts=512):
    b, s, d = xb.shape
    ts = min(ts, s)
    w_ab = jnp.pad(w_ab, ((0, 0), (0, LANES - w_ab.shape[1])))
    out = jax.ShapeDtypeStruct((b, N_GDN_V_HEADS, s), _F32)
    return pl.pallas_call(
        _gdn_gate_kernel,
        grid=(b, s // ts),
        in_specs=[pl.BlockSpec((1, ts, d), lambda i, j: (i, j, 0)),
                  pl.BlockSpec((d, LANES), lambda i, j: (0, 0)),
                  pl.BlockSpec((N_GDN_V_HEADS, 1), lambda i, j: (0, 0)),
                  pl.BlockSpec((N_GDN_V_HEADS, 1), lambda i, j: (0, 0))],
        out_specs=[pl.BlockSpec((1, N_GDN_V_HEADS, ts), lambda i, j: (i, 0, j))] * 2,
        out_shape=[out, out],
        compiler_params=_cparams(("parallel", "parallel")),
        name="gdn_gates",
    )(xb, w_ab, a_log.reshape(-1, 1).astype(_F32), dt_bias.reshape(-1, 1).astype(_F32))


def _unit_lower_inverses(lows, ii, jj):
    n = lows[0].shape[0]
    xor = ii ^ jj
    eye = (ii == jj).astype(_F32)
    d1s = [jnp.where(xor < 8, low, 0.0) for low in lows]
    d1bs = [d1.astype(_BF16) for d1 in d1s]
    d2bs = [_dot(d1b, d1b).astype(_BF16) for d1b in d1bs]
    yield None
    ts = [eye - d1 for d1 in d1s]
    ts = [t + _dot(t.astype(_BF16), d2b) for t, d2b in zip(ts, d2bs)]
    yield None
    d4bs = [_dot(d2b, d2b).astype(_BF16) for d2b in d2bs]
    yield None
    ts = [t + _dot(t.astype(_BF16), d4b) for t, d4b in zip(ts, d4bs)]
    yield None
    neg_lows = [(-low).astype(_BF16) for low in lows]
    size = 8
    while size < n:
        band = (xor >= size) & (xor < 2 * size)
        tbs = [t.astype(_BF16) for t in ts]
        ys = [_dot(tb, nl).astype(_BF16) for tb, nl in zip(tbs, neg_lows)]
        yield None
        ts = [jnp.where(band, _dot(y, tb), t) for t, y, tb in zip(ts, ys, tbs)]
        yield None
        size *= 2
    yield ts


def _conv_silu_rows(buf_ref, w, r0, n):
    y = None
    for lag in range(CONV_WIDTH):
        lo = CONV_HALO + r0 - lag
        term = buf_ref[lo:lo + n, :] * w[CONV_WIDTH - 1 - lag:CONV_WIDTH - lag, :]
        y = term if y is None else y + term
    return _silu(y)


def _gdn_kernel(blocks_per_seq, q_ref, k_ref, v_ref, wq_ref, wk_ref, wv_ref, gc_ref, beta_ref, ng_ref, o_ref,
                qbuf, kbuf, vbuf, state_ref, low_s, t_s, kp_s, qg_s, rhs_s, gl_s):
    rows = q_ref.shape[1]
    n = GDN_CHUNK
    dv = GDN_V_HEAD_DIM
    n_sys = 2 * (rows // n)
    g_step = pl.program_id(0)
    slot = g_step % 2

    @pl.when(g_step == 0)
    def _():
        for ref in (low_s, t_s, kp_s, qg_s, rhs_s, gl_s, state_ref):
            ref[...] = jnp.zeros_like(ref)

    @pl.when(g_step % blocks_per_seq == 0)
    def _():
        for ref in (qbuf, kbuf, vbuf):
            ref[0:CONV_HALO, :] = jnp.zeros((CONV_HALO, ref.shape[1]), _F32)

    ii = lax.broadcasted_iota(jnp.int32, (n, n), 0)
    jj = lax.broadcasted_iota(jnp.int32, (n, n), 1)
    causal = ii >= jj
    strict = ii > jj

    def recurrence_stage():
        fresh = (g_step - 2) % blocks_per_seq == 0
        ng = ng_ref[...]
        prods = []
        for idx in range(n_sys):
            wu = _dot(t_s[idx], rhs_s[slot, idx]).astype(_BF16)
            both = _dot(kp_s[slot, idx], wu)
            q_eff = (qg_s[slot, idx] - both[n:, :n]).astype(_BF16)
            prods.append((jnp.concatenate([q_eff, both[:n, :n].astype(_BF16)], 0), both[:n, n:],
                          both[n:, n:], gl_s[slot, idx, 0:1, :]))
            yield
        states = [jnp.where(fresh, 0.0, state_ref[e]) for e in range(2)]
        for idx, (qa, n_mat, pu, gl) in enumerate(prods):
            c, e = divmod(idx, 2)
            qa_s = _dot(qa, states[e].astype(_BF16))
            out = qa_s[:n] + pu
            states[e] = states[e] * gl - qa_s[n:] + n_mat
            out = out * lax.rsqrt(jnp.mean(out * out, -1, keepdims=True) + NORM_EPS) * ng
            o_ref[0, c * n:(c + 1) * n, e * dv:(e + 1) * dv] = out.astype(o_ref.dtype)
            yield
        state_ref[0] = states[0]
        state_ref[1] = states[1]

    def inverse_stage():
        for t_invs in _unit_lower_inverses([low_s[idx] for idx in range(n_sys)], ii, jj):
            yield
        for idx, t_inv in enumerate(t_invs):
            t_s[idx] = t_inv.astype(_BF16)

    def operand_stage():
        bufs = ((qbuf, q_ref), (kbuf, k_ref), (vbuf, v_ref))
        for buf, ref in bufs:
            buf[CONV_HALO:CONV_HALO + rows, :] = ref[0].astype(_F32)
        yield
        wq, wk, wv = wq_ref[...], wk_ref[...], wv_ref[...]
        for c in range(rows // n):
            r0 = c * n
            qc = _conv_silu_rows(qbuf, wq, r0, n)
            kc = _conv_silu_rows(kbuf, wk, r0, n)
            qn = qc * (lax.rsqrt(jnp.sum(qc * qc, -1, keepdims=True) + NORM_EPS) * (QK_HEAD_DIM ** -0.5))
            kn = kc * lax.rsqrt(jnp.sum(kc * kc, -1, keepdims=True) + NORM_EPS)
            kb = kn.astype(_BF16)
            kq = _dot_nt(jnp.concatenate([kb, qn.astype(_BF16)], 0), kb)
            kk, qk = kq[:n], kq[n:]
            yield
            vc = _conv_silu_rows(vbuf, wv, r0, n)
            for e in range(2):
                idx = 2 * c + e
                g_row = jnp.broadcast_to(gc_ref[0, 0, e:e + 1, r0:r0 + n], (n, n))
                g_col = g_row.T
                b_col = jnp.broadcast_to(beta_ref[0, 0, e:e + 1, r0:r0 + n], (n, n)).T
                decay = jnp.where(causal, jnp.exp(jnp.where(causal, g_col - g_row, 0.0)), 0.0)
                e_g = jnp.exp(g_col)
                g_last = g_col[n - 1:n, :]
                v_e = vc[:, e * dv:(e + 1) * dv]
                low_s[idx] = jnp.where(strict, b_col * kk * decay, 0.0)
                k_tail_t = (kn * jnp.exp(g_last - g_col)).T
                kp_s[slot, idx] = jnp.concatenate([k_tail_t, qk * decay], 0).astype(_BF16)
                qg_s[slot, idx] = qn * e_g
                rhs_s[slot, idx] = jnp.concatenate([b_col * e_g * kn, b_col * v_e], 1).astype(_BF16)
                gl_s[slot, idx] = jnp.broadcast_to(jnp.exp(g_last), (CONV_HALO, dv))
                yield
        for buf, _ in bufs:
            buf[0:CONV_HALO, :] = buf[rows:rows + CONV_HALO, :]

    n_chunks = rows // n
    _interleave((recurrence_stage(), 2 * n_sys + 1), (inverse_stage(), 13), (operand_stage(), 3 * n_chunks + 2))


def _gdn(h, conv_w, gc, beta, norm_g, rows=GDN_ROWS):
    b, s, _ = h.shape
    rows = min(rows, s)
    nb = s // rows
    total = b * N_QK_HEADS * nb
    v_blk0 = 2 * QK_WIDTH // (2 * GDN_V_HEAD_DIM)
    gc = gc.reshape(b, N_QK_HEADS, 2, s)
    beta = beta.reshape(b, N_QK_HEADS, 2, s)
    pair = 2 * GDN_V_HEAD_DIM
    n_sys = 2 * (rows // GDN_CHUNK)

    def where(g):
        g = jnp.minimum(g, total - 1)
        return g // (N_QK_HEADS * nb), (g // nb) % N_QK_HEADS, g % nb

    def cur(col0):
        def index(g):
            i, hd, t = where(g)
            return i, t, col0 + hd
        return index

    def gate_index(g):
        i, hd, t = where(g)
        return i, hd, 0, t

    def out_index(g):
        i, hd, t = where(jnp.maximum(g - 2, 0))
        return i, t, hd

    return pl.pallas_call(
        functools.partial(_gdn_kernel, nb),
        grid=(total + 2,),
        in_specs=[
            pl.BlockSpec((1, rows, QK_HEAD_DIM), cur(0)),
            pl.BlockSpec((1, rows, QK_HEAD_DIM), cur(N_QK_HEADS)),
            pl.BlockSpec((1, rows, pair), cur(v_blk0)),
            pl.BlockSpec((CONV_WIDTH, QK_HEAD_DIM), lambda g: (0, where(g)[1])),
            pl.BlockSpec((CONV_WIDTH, QK_HEAD_DIM), lambda g: (0, N_QK_HEADS + where(g)[1])),
            pl.BlockSpec((CONV_WIDTH, pair), lambda g: (0, v_blk0 + where(g)[1])),
            pl.BlockSpec((1, 1, 2, rows), gate_index),
            pl.BlockSpec((1, 1, 2, rows), gate_index),
            pl.BlockSpec((1, GDN_V_HEAD_DIM), lambda g: (0, 0)),
        ],
        out_specs=pl.BlockSpec((1, rows, pair), out_index),
        out_shape=jax.ShapeDtypeStruct((b, s, TOK_WIDTH), _BF16),
        scratch_shapes=[pltpu.VMEM((rows + CONV_HALO, QK_HEAD_DIM), _F32),
                        pltpu.VMEM((rows + CONV_HALO, QK_HEAD_DIM), _F32),
                        pltpu.VMEM((rows + CONV_HALO, pair), _F32),
                        pltpu.VMEM((2, QK_HEAD_DIM, GDN_V_HEAD_DIM), _F32),
                        pltpu.VMEM((n_sys, GDN_CHUNK, GDN_CHUNK), _F32),
                        pltpu.VMEM((n_sys, GDN_CHUNK, GDN_CHUNK), _BF16),
                        pltpu.VMEM((2, n_sys, QK_HEAD_DIM + GDN_CHUNK, GDN_CHUNK), _BF16),
                        pltpu.VMEM((2, n_sys, GDN_CHUNK, QK_HEAD_DIM), _F32),
                        pltpu.VMEM((2, n_sys, GDN_CHUNK, QK_HEAD_DIM + GDN_V_HEAD_DIM), _BF16),
                        pltpu.VMEM((2, n_sys, CONV_HALO, GDN_V_HEAD_DIM), _F32)],
        compiler_params=_cparams(("arbitrary",)),
        name="gated_delta",
    )(h, h, h, conv_w, conv_w, conv_w, gc, beta, norm_g.reshape(1, GDN_V_HEAD_DIM))


def _out_kernel(tok_ref, mq_ref, z0_ref, z1_ref, z2_ref, z3_ref, kv_ref, w_ref, x_ref, g_ref, b_ref,
                o_ref, ob_ref, y_s):
    tm, d = x_ref.shape
    kc = MIX_WIDTH // OUT_K_SPLIT
    z_refs = (z0_ref, z1_ref, z2_ref, z3_ref)

    @pl.when(pl.program_id(0) == 0)
    def _():
        y_s[...] = jnp.zeros_like(y_s)

    def attend(m):
        lo = m * MEM_HEAD_DIM
        mk = kv_ref[0, :, lo:lo + MEM_HEAD_DIM]
        mv = kv_ref[0, :, MEM_WIDTH + lo:MEM_WIDTH + lo + MEM_HEAD_DIM]
        sc = _dot_nt(mq_ref[:, lo:lo + MEM_HEAD_DIM], mk) * (MEM_HEAD_DIM ** -0.5)
        p = jnp.exp(sc - jnp.max(sc, -1, keepdims=True))
        p = p / jnp.sum(p, -1, keepdims=True)
        return _dot(p.astype(_BF16), mv)

    def branch_chunk(k):
        lo = k * kc
        z = z_refs[lo // PROJ_TILE][:, lo % PROJ_TILE:lo % PROJ_TILE + kc].astype(_F32)
        if lo < TOK_WIDTH:
            val = tok_ref[:, lo:lo + kc].astype(_F32)
        else:
            first = (lo - TOK_WIDTH) // MEM_HEAD_DIM
            val = jnp.concatenate([attend(first + j) for j in range(kc // MEM_HEAD_DIM)], 1)
        return (val * _silu(z)).astype(_BF16)

    def project_stage():
        acc = None
        for k in range(OUT_K_SPLIT):
            part = _dot(branch_chunk(k), w_ref[0, k * kc:(k + 1) * kc, :])
            acc = part if acc is None else acc + part
            yield
        y_s[...] = acc

    def norm_stage():
        rows = tm // OUT_K_SPLIT
        for blk in range(OUT_K_SPLIT):
            sl = slice(blk * rows, (blk + 1) * rows)
            r = DEEPNORM_ALPHA * x_ref[sl, :] + y_s[sl, :]
            mu = jnp.mean(r, -1, keepdims=True)
            dev = r - mu
            var = jnp.mean(dev * dev, -1, keepdims=True)
            out = dev * lax.rsqrt(var + LN_EPS) * g_ref[...] + b_ref[...]
            o_ref[sl, :] = out
            ob_ref[sl, :] = out.astype(_BF16)
            yield

    _interleave((norm_stage(), OUT_K_SPLIT), (project_stage(), OUT_K_SPLIT + 1))


def _gate_out_ln(tok, h, kv, w_out_b, layer, x, ln_g, ln_b, seq, tm=OUT_ROWS):
    m, d = x.shape
    tm = min(tm, seq)
    n_tiles = m // tm
    n_mem = kv.shape[1]
    z_blk0 = (CONV_CH + MEM_WIDTH) // PROJ_TILE
    once = pl.Buffered(1)

    def gated(i):
        return jnp.minimum(i, n_tiles - 1)

    def normed(i):
        return jnp.maximum(i - 1, 0)

    def h_cols(blk):
        return pl.BlockSpec((tm, PROJ_TILE), lambda i: (gated(i), blk))

    return pl.pallas_call(
        _out_kernel,
        grid=(n_tiles + 1,),
        in_specs=[pl.BlockSpec((tm, TOK_WIDTH), lambda i: (gated(i), 0)),
                  h_cols(CONV_CH // PROJ_TILE),
                  h_cols(z_blk0), h_cols(z_blk0 + 1), h_cols(z_blk0 + 2), h_cols(z_blk0 + 3),
                  pl.BlockSpec((1, n_mem, 2 * MEM_WIDTH), lambda i: (gated(i) * tm // seq, 0, 0)),
                  pl.BlockSpec((1, MIX_WIDTH, d), lambda i: (layer, 0, 0), pipeline_mode=once),
                  pl.BlockSpec((tm, d), lambda i: (normed(i), 0)),
                  pl.BlockSpec((1, d), lambda i: (0, 0)),
                  pl.BlockSpec((1, d), lambda i: (0, 0))],
        out_specs=[pl.BlockSpec((tm, d), lambda i: (normed(i), 0))] * 2,
        out_shape=[jax.ShapeDtypeStruct((m, d), _F32), jax.ShapeDtypeStruct((m, d), _BF16)],
        scratch_shapes=[pltpu.VMEM((tm, d), _F32)],
        compiler_params=_cparams(("arbitrary",)),
        name="gate_out_layernorm",
    )(tok, h, h, h, h, h, kv, w_out_b, x, ln_g.reshape(1, d), ln_b.reshape(1, d))


def kernel(x, mem, positions, w_in_ret, ret_norm_g, w_in_gdn, conv_w, a_log, dt_bias, gdn_norm_g,
           w_mem_kv, w_out, ln_g, ln_b):
    b, s, d = x.shape
    m = b * s
    n_mem = mem.shape[1]
    cos, sin = _rope_tables(positions)
    mem_b = mem.reshape(b * n_mem, d).astype(_BF16)
    w_out_b = w_out.astype(_BF16)
    w_in_gdn_t = jnp.swapaxes(w_in_gdn, 1, 2)
    xf = x.reshape(m, d)
    xb = xf.astype(_BF16)
    for i in range(DEPTH):
        j = i // 2
        if i % 2 == 0:
            h = _project(xb, w_in_ret, j, RET_COLS)
            tok = _retention(h.reshape(b, s, RET_COLS), cos, sin, ret_norm_g[j])
        else:
            h = _project(xb, w_in_gdn_t, j, RET_COLS, w_is_transposed=True)
            gc, beta = _gdn_gates(xb.reshape(b, s, d), w_in_gdn[j, :, RET_COLS:], a_log[j], dt_bias[j])
            tok = _gdn(h.reshape(b, s, RET_COLS), conv_w[j], gc, beta, gdn_norm_g[j])
        kv = _project(mem_b, w_mem_kv, i, 2 * MEM_WIDTH).reshape(b, n_mem, 2 * MEM_WIDTH)
        xf, xb = _gate_out_ln(tok.reshape(m, TOK_WIDTH), h, kv, w_out_b, i, xf, ln_g[i], ln_b[i], s)
    return xf.reshape(b, s, d)
```

```python
import functools

import numpy as np
import jax
import jax.numpy as jnp
from jax import lax
from jax.experimental import pallas as pl
from jax.experimental.pallas import tpu as pltpu

QK_HEAD_DIM = 128
N_QK_HEADS = 12
QK_WIDTH = N_QK_HEADS * QK_HEAD_DIM
TOK_WIDTH = 3072
RET_V_HEAD_DIM = 256
GDN_V_HEAD_DIM = 128
N_GDN_V_HEADS = 24
MEM_HEADS = 4
MEM_HEAD_DIM = 256
MEM_WIDTH = MEM_HEADS * MEM_HEAD_DIM
MIX_WIDTH = TOK_WIDTH + MEM_WIDTH
CONV_WIDTH = 4
CONV_CH = 2 * QK_WIDTH + TOK_WIDTH
RET_COLS = CONV_CH + MEM_WIDTH + MIX_WIDTH
ROPE_BASE = 10000.0
DEPTH = 4
DEEPNORM_ALPHA = (2.0 * DEPTH) ** 0.25
LN_EPS = 1e-5
NORM_EPS = 1e-6

V7X_VMEM_BYTES = 64 * 1024 * 1024
VMEM_LIMIT = V7X_VMEM_BYTES - 8 * 1024 * 1024
LANES = 128

PROJ_TILE = 1024
PROJ_ROWS = 2048
RET_CHUNK = 512
RET_HEADS_PER_STEP = 4
GDN_CHUNK = 128
GDN_ROWS = 512
CONV_HALO = 8
OUT_ROWS = 256
OUT_K_SPLIT = 4

_F32 = jnp.float32
_BF16 = jnp.bfloat16


def _cparams(sem):
    return pltpu.CompilerParams(dimension_semantics=sem, vmem_limit_bytes=VMEM_LIMIT)


def _dot(a, b):
    return jnp.dot(a, b, preferred_element_type=_F32)


def _dot_nt(a, b):
    return lax.dot_general(a, b, (((1,), (1,)), ((), ())), preferred_element_type=_F32)


def _dot_tn(a, b):
    return lax.dot_general(a, b, (((0,), (0,)), ((), ())), preferred_element_type=_F32)


def _silu(z):
    return z * jax.nn.sigmoid(z)


def _interleave(*stages):
    live = [[gen, count, 0] for gen, count in stages]
    while live:
        stage = min(live, key=lambda st: (st[2] + 1) / st[1])
        try:
            next(stage[0])
            stage[2] += 1
        except StopIteration:
            live.remove(stage)


def _proj_kernel(w_is_transposed, x_ref, w_ref, o_ref, wb_ref):
    @pl.when(pl.program_id(1) == 0)
    def _():
        w = w_ref[0]
        wb_ref[...] = (w.T if w_is_transposed else w).astype(_BF16)

    o_ref[...] = _dot(x_ref[...], wb_ref[...]).astype(o_ref.dtype)


def _project(x, w_stack, layer, n_cols, w_is_transposed=False, tile=PROJ_TILE, rows=PROJ_ROWS):
    m, k = x.shape
    tm = min(rows, m)
    assert m % tm == 0 and n_cols % tile == 0
    col_tiles = n_cols // tile
    n_layers = w_stack.shape[0] if layer is None else 1

    def which(j):
        return (j // col_tiles, j % col_tiles) if layer is None else (layer, j)

    if w_is_transposed:
        w_spec = pl.BlockSpec((1, tile, k), lambda j, i: (which(j)[0], which(j)[1], 0))
    else:
        w_spec = pl.BlockSpec((1, k, tile), lambda j, i: (which(j)[0], 0, which(j)[1]))
    return pl.pallas_call(
        functools.partial(_proj_kernel, w_is_transposed),
        grid=(n_layers * col_tiles, m // tm),
        in_specs=[pl.BlockSpec((tm, k), lambda j, i: (i, 0)), w_spec],
        out_specs=pl.BlockSpec((tm, tile), lambda j, i: (i, j)),
        out_shape=jax.ShapeDtypeStruct((m, n_layers * n_cols), _BF16),
        scratch_shapes=[pltpu.VMEM((k, tile), _BF16)],
        compiler_params=_cparams(("parallel", "arbitrary")),
        name="proj_matmul",
    )(x, w_stack)


def _rope_kernel(pos_ref, freq_ref, cos_ref, sin_ref):
    ang = pos_ref[0].astype(_F32) * freq_ref[...]
    lane = lax.broadcasted_iota(jnp.int32, ang.shape, 1)
    s = jnp.sin(ang)
    cos_ref[0] = jnp.cos(ang)
    sin_ref[0] = jnp.where(lane < QK_HEAD_DIM // 2, -s, s)


def _rope_tables(positions, ts=512):
    b, s = positions.shape
    half = QK_HEAD_DIM // 2
    inv_freq = (ROPE_BASE ** (-np.arange(half, dtype=np.float32) / np.float32(half))).astype(np.float32)
    freq = jnp.asarray(np.concatenate([inv_freq, inv_freq])[None, :])
    ts = min(ts, s)
    out = jax.ShapeDtypeStruct((b, s, QK_HEAD_DIM), _F32)
    return pl.pallas_call(
        _rope_kernel,
        grid=(b, s // ts),
        in_specs=[pl.BlockSpec((1, ts, 1), lambda i, j: (i, j, 0)),
                  pl.BlockSpec((1, QK_HEAD_DIM), lambda i, j: (0, 0))],
        out_specs=[pl.BlockSpec((1, ts, QK_HEAD_DIM), lambda i, j: (i, j, 0))] * 2,
        out_shape=[out, out],
        compiler_params=_cparams(("parallel", "parallel")),
        name="rope_tables",
    )(positions.reshape(b, s, 1), freq)


def _retention_kernel(lg_ref, q_ref, k_ref, v_ref, cos_ref, sin_ref, g_ref, o_ref, state_ref, decay_ref):
    c = q_ref.shape[1]
    dk, dv = QK_HEAD_DIM, RET_V_HEAD_DIM
    pair = pl.program_id(1)

    @pl.when(pl.program_id(2) == 0)
    def _():
        state_ref[...] = jnp.zeros_like(state_ref)
        ii = lax.broadcasted_iota(jnp.int32, (c, c), 0)
        jj = lax.broadcasted_iota(jnp.int32, (c, c), 1)
        rel = (ii - jj).astype(_F32)
        for j in range(RET_HEADS_PER_STEP):
            lg = lg_ref[RET_HEADS_PER_STEP * pair + j]
            decay_ref[j] = jnp.where(rel >= 0, jnp.exp(lg * jnp.maximum(rel, 0.0)), 0.0)

    cos = cos_ref[0]
    sin = sin_ref[0]
    idx = lax.broadcasted_iota(jnp.int32, (c, dk), 0).astype(_F32)

    def head(j):
        lg = lg_ref[RET_HEADS_PER_STEP * pair + j]
        q = q_ref[0, :, j * dk:(j + 1) * dk].astype(_F32)
        k = k_ref[0, :, j * dk:(j + 1) * dk].astype(_F32)
        qr = q * cos + pltpu.roll(q, dk // 2, 1) * sin
        kr = (k * cos + pltpu.roll(k, dk // 2, 1) * sin) * (dk ** -0.5)
        qb = qr.astype(_BF16)
        kb = kr.astype(_BF16)
        q_in = (qr * jnp.exp((idx + 1.0) * lg)).astype(_BF16)
        k_out = (kr * jnp.exp((c - 1.0 - idx) * lg)).astype(_BF16)
        yield
        scores = (_dot_nt(qb, kb) * decay_ref[j]).astype(_BF16)
        yield
        vb = v_ref[0, :, j * dv:(j + 1) * dv]
        state = state_ref[j]
        o = _dot(jnp.concatenate([scores, q_in], 1), jnp.concatenate([vb, state.astype(_BF16)], 0))
        state_ref[j] = state * jnp.exp(c * lg) + _dot_tn(k_out, vb)
        yield
        mu = jnp.mean(o, -1, keepdims=True)
        d = o - mu
        var = jnp.mean(d * d, -1, keepdims=True)
        o_ref[0, :, j * dv:(j + 1) * dv] = (
            d * lax.rsqrt(var + NORM_EPS) * g_ref[:, j * dv:(j + 1) * dv]).astype(o_ref.dtype)

    _interleave(*[(head(j), 4) for j in range(RET_HEADS_PER_STEP)])


def _retention(h, cos, sin, norm_g, c=RET_CHUNK):
    b, s, _ = h.shape
    c = min(c, s)
    hp = RET_HEADS_PER_STEP
    log_gamma = jnp.asarray(np.log1p(-np.exp2(-5.0 - np.arange(N_QK_HEADS, dtype=np.float64))).astype(np.float32))
    n_pairs = N_QK_HEADS // hp
    v_blk0 = 2 * QK_WIDTH // (hp * RET_V_HEAD_DIM)
    grid_spec = pltpu.PrefetchScalarGridSpec(
        num_scalar_prefetch=1,
        grid=(b, n_pairs, s // c),
        in_specs=[
            pl.BlockSpec((1, c, hp * QK_HEAD_DIM), lambda i, hd, t, lg: (i, t, hd)),
            pl.BlockSpec((1, c, hp * QK_HEAD_DIM), lambda i, hd, t, lg: (i, t, n_pairs + hd)),
            pl.BlockSpec((1, c, hp * RET_V_HEAD_DIM), lambda i, hd, t, lg: (i, t, v_blk0 + hd)),
            pl.BlockSpec((1, c, QK_HEAD_DIM), lambda i, hd, t, lg: (i, t, 0)),
            pl.BlockSpec((1, c, QK_HEAD_DIM), lambda i, hd, t, lg: (i, t, 0)),
            pl.BlockSpec((1, hp * RET_V_HEAD_DIM), lambda i, hd, t, lg: (0, hd)),
        ],
        out_specs=pl.BlockSpec((1, c, hp * RET_V_HEAD_DIM), lambda i, hd, t, lg: (i, t, hd)),
        scratch_shapes=[pltpu.VMEM((hp, QK_HEAD_DIM, RET_V_HEAD_DIM), _F32),
                        pltpu.VMEM((hp, c, c), _F32)],
    )
    return pl.pallas_call(
        _retention_kernel,
        grid_spec=grid_spec,
        out_shape=jax.ShapeDtypeStruct((b, s, TOK_WIDTH), _BF16),
        compiler_params=_cparams(("parallel", "parallel", "arbitrary")),
        name="retention",
    )(log_gamma, h, h, h, cos, sin, norm_g.reshape(1, TOK_WIDTH))


def _gdn_gate_kernel(x_ref, w_ref, alog_ref, dtb_ref, gb_ref):
    ab = _dot_nt(w_ref[...].T.astype(_BF16), x_ref[0])
    a = ab[:N_GDN_V_HEADS]
    bb = ab[N_GDN_V_HEADS:2 * N_GDN_V_HEADS]
    g = -jnp.exp(alog_ref[...]) * jax.nn.softplus(a + dtb_ref[...])
    lane = lax.broadcasted_iota(jnp.int32, g.shape, 1) % GDN_CHUNK
    shift = 1
    while shift < GDN_CHUNK:
        g = g + jnp.where(lane >= shift, pltpu.roll(g, shift, 1), 0.0)
        shift *= 2
    gb_ref[0, 0] = g
    gb_ref[0, 1] = jax.nn.sigmoid(bb)


def _gdn_gates(xb, w_ab, a_log, dt_bias, ts=2048):
    b, s, d = xb.shape
    ts = min(ts, s)
    w_ab = jnp.pad(w_ab, ((0, 0), (0, LANES - w_ab.shape[1])))
    return pl.pallas_call(
        _gdn_gate_kernel,
        grid=(b, s // ts),
        in_specs=[pl.BlockSpec((1, ts, d), lambda i, j: (i, j, 0)),
                  pl.BlockSpec((d, LANES), lambda i, j: (0, 0)),
                  pl.BlockSpec((N_GDN_V_HEADS, 1), lambda i, j: (0, 0)),
                  pl.BlockSpec((N_GDN_V_HEADS, 1), lambda i, j: (0, 0))],
        out_specs=pl.BlockSpec((1, 2, N_GDN_V_HEADS, ts), lambda i, j: (i, 0, 0, j)),
        out_shape=jax.ShapeDtypeStruct((b, 2, N_GDN_V_HEADS, s), _F32),
        compiler_params=_cparams(("parallel", "parallel")),
        name="gdn_gates",
    )(xb, w_ab, a_log.reshape(-1, 1).astype(_F32), dt_bias.reshape(-1, 1).astype(_F32))


def _unit_lower_inverses(lows, ii, jj):
    n = lows[0].shape[0]
    xor = ii ^ jj
    eye = (ii == jj).astype(_F32)
    d1s = [jnp.where(xor < 8, low, 0.0) for low in lows]
    d1bs = [d1.astype(_BF16) for d1 in d1s]
    d2bs = [_dot(d1b, d1b).astype(_BF16) for d1b in d1bs]
    yield None
    ts = [eye - d1 for d1 in d1s]
    ts = [t + _dot(t.astype(_BF16), d2b) for t, d2b in zip(ts, d2bs)]
    yield None
    d4bs = [_dot(d2b, d2b).astype(_BF16) for d2b in d2bs]
    yield None
    ts = [t + _dot(t.astype(_BF16), d4b) for t, d4b in zip(ts, d4bs)]
    yield None
    neg_lows = [-low.astype(_BF16) for low in lows]
    tbs = [t.astype(_BF16) for t in ts]
    size = 8
    while size < n:
        band = (xor >= size) & (xor < 2 * size)
        ys = [_dot(tb, nl).astype(_BF16) for tb, nl in zip(tbs, neg_lows)]
        yield None
        tbs = [jnp.where(band, _dot(y, tb).astype(_BF16), tb) for y, tb in zip(ys, tbs)]
        yield None
        size *= 2
    yield tbs


def _conv_silu_rows(buf_ref, w, r0, n):
    y = None
    for lag in range(CONV_WIDTH):
        lo = CONV_HALO + r0 - lag
        term = buf_ref[lo:lo + n, :] * w[CONV_WIDTH - 1 - lag:CONV_WIDTH - lag, :]
        y = term if y is None else y + term
    return _silu(y)


def _gdn_kernel(blocks_per_seq, n_blocks, q_ref, k_ref, v_ref, cw_ref, gb_ref, ng_ref, o_ref,
                qbuf, kbuf, vbuf, state_ref, low_s, t_s, kp_s, qg_s, rhs_s, gl_s):
    rows = q_ref.shape[1]
    n = GDN_CHUNK
    dv = GDN_V_HEAD_DIM
    n_sys = 2 * (rows // n)
    g_step = pl.program_id(0)
    slot = g_step % 2

    @pl.when(g_step == 0)
    def _():
        for ref in (low_s, t_s, kp_s, qg_s, rhs_s, gl_s, state_ref):
            ref[...] = jnp.zeros_like(ref)

    @pl.when(g_step % blocks_per_seq == 0)
    def _():
        for ref in (qbuf, kbuf, vbuf):
            ref[0:CONV_HALO, :] = jnp.zeros((CONV_HALO, ref.shape[1]), _F32)

    ii = lax.broadcasted_iota(jnp.int32, (n, n), 0)
    jj = lax.broadcasted_iota(jnp.int32, (n, n), 1)
    causal = ii >= jj
    strict = ii > jj

    def recurrence_stage():
        fresh = (g_step - 2) % blocks_per_seq == 0
        ng = ng_ref[...]
        prods = []
        for idx in range(n_sys):
            wu = _dot(t_s[idx], rhs_s[slot, idx]).astype(_BF16)
            both = _dot(kp_s[slot, idx], wu)
            q_eff = (qg_s[slot, idx] - both[n:, :n]).astype(_BF16)
            prods.append((jnp.concatenate([q_eff, both[:n, :n].astype(_BF16)], 0), both[:n, n:],
                          both[n:, n:], gl_s[slot, idx, 0:1, :]))
            yield
        states = [jnp.where(fresh, 0.0, state_ref[e]) for e in range(2)]
        for idx, (qa, n_mat, pu, gl) in enumerate(prods):
            c, e = divmod(idx, 2)
            qa_s = _dot(qa, states[e].astype(_BF16))
            out = qa_s[:n] + pu
            states[e] = states[e] * gl - qa_s[n:] + n_mat
            out = out * lax.rsqrt(jnp.mean(out * out, -1, keepdims=True) + NORM_EPS) * ng
            o_ref[0, c * n:(c + 1) * n, e * dv:(e + 1) * dv] = out.astype(o_ref.dtype)
            yield
        state_ref[0] = states[0]
        state_ref[1] = states[1]

    def inverse_stage():
        for t_invs in _unit_lower_inverses([low_s[idx] for idx in range(n_sys)], ii, jj):
            yield
        for idx, t_inv in enumerate(t_invs):
            t_s[idx] = t_inv

    def operand_stage():
        bufs = ((qbuf, q_ref), (kbuf, k_ref), (vbuf, v_ref))
        for buf, ref in bufs:
            buf[CONV_HALO:CONV_HALO + rows, :] = ref[0].astype(_F32)
        yield
        head = (jnp.minimum(g_step, n_blocks - 1) // blocks_per_seq) % N_QK_HEADS
        wq = cw_ref[head]
        wk = cw_ref[N_QK_HEADS + head]
        wv = jnp.concatenate([cw_ref[2 * N_QK_HEADS + 2 * head], cw_ref[2 * N_QK_HEADS + 2 * head + 1]], 1)
        for c in range(rows // n):
            r0 = c * n
            qc = _conv_silu_rows(qbuf, wq, r0, n)
            kc = _conv_silu_rows(kbuf, wk, r0, n)
            qn = qc * (lax.rsqrt(jnp.sum(qc * qc, -1, keepdims=True) + NORM_EPS) * (QK_HEAD_DIM ** -0.5))
            kn = kc * lax.rsqrt(jnp.sum(kc * kc, -1, keepdims=True) + NORM_EPS)
            kb = kn.astype(_BF16)
            kq = _dot_nt(jnp.concatenate([kb, qn.astype(_BF16)], 0), kb)
            kk, qk = kq[:n], kq[n:]
            yield
            vc = _conv_silu_rows(vbuf, wv, r0, n)
            for e in range(2):
                idx = 2 * c + e
                g_row = jnp.broadcast_to(gb_ref[0, 0, 0, e:e + 1, r0:r0 + n], (n, n))
                g_col = g_row.T
                b_col = jnp.broadcast_to(gb_ref[0, 1, 0, e:e + 1, r0:r0 + n], (n, n)).T
                decay = jnp.where(causal, jnp.exp(g_col - g_row), 0.0)
                e_g = jnp.exp(g_col)
                g_last = g_col[n - 1:n, :]
                v_e = vc[:, e * dv:(e + 1) * dv]
                low_s[idx] = jnp.where(strict, b_col * kk * decay, 0.0)
                k_tail_t = (kn * jnp.exp(g_last - g_col)).T
                kp_s[slot, idx] = jnp.concatenate([k_tail_t, qk * decay], 0).astype(_BF16)
                qg_s[slot, idx] = qn * e_g
                rhs_s[slot, idx] = jnp.concatenate([b_col * e_g * kn, b_col * v_e], 1).astype(_BF16)
                gl_s[slot, idx] = jnp.broadcast_to(jnp.exp(g_last), (CONV_HALO, dv))
                yield
        for buf, _ in bufs:
            buf[0:CONV_HALO, :] = buf[rows:rows + CONV_HALO, :]

    n_chunks = rows // n
    _interleave((recurrence_stage(), 2 * n_sys + 1), (inverse_stage(), 13), (operand_stage(), 3 * n_chunks + 2))


def _gdn(h, conv_w, gates, norm_g, rows=GDN_ROWS):
    b, s, _ = h.shape
    rows = min(rows, s)
    nb = s // rows
    total = b * N_QK_HEADS * nb
    v_blk0 = 2 * QK_WIDTH // (2 * GDN_V_HEAD_DIM)
    gates = gates.reshape(b, 2, N_QK_HEADS, 2, s)
    conv_groups = conv_w.reshape(CONV_WIDTH, CONV_CH // LANES, LANES).transpose(1, 0, 2)
    pair = 2 * GDN_V_HEAD_DIM
    n_sys = 2 * (rows // GDN_CHUNK)

    def where(g):
        g = jnp.minimum(g, total - 1)
        return g // (N_QK_HEADS * nb), (g // nb) % N_QK_HEADS, g % nb

    def cur(col0):
        def index(g):
            i, hd, t = where(g)
            return i, t, col0 + hd
        return index

    def gate_index(g):
        i, hd, t = where(g)
        return i, 0, hd, 0, t

    def out_index(g):
        i, hd, t = where(jnp.maximum(g - 2, 0))
        return i, t, hd

    return pl.pallas_call(
        functools.partial(_gdn_kernel, nb, total),
        grid=(total + 2,),
        in_specs=[
            pl.BlockSpec((1, rows, QK_HEAD_DIM), cur(0)),
            pl.BlockSpec((1, rows, QK_HEAD_DIM), cur(N_QK_HEADS)),
            pl.BlockSpec((1, rows, pair), cur(v_blk0)),
            pl.BlockSpec((CONV_CH // LANES, CONV_WIDTH, LANES), lambda g: (0, 0, 0)),
            pl.BlockSpec((1, 2, 1, 2, rows), gate_index),
            pl.BlockSpec((1, GDN_V_HEAD_DIM), lambda g: (0, 0)),
        ],
        out_specs=pl.BlockSpec((1, rows, pair), out_index),
        out_shape=jax.ShapeDtypeStruct((b, s, TOK_WIDTH), _BF16),
        scratch_shapes=[pltpu.VMEM((rows + CONV_HALO, QK_HEAD_DIM), _F32),
                        pltpu.VMEM((rows + CONV_HALO, QK_HEAD_DIM), _F32),
                        pltpu.VMEM((rows + CONV_HALO, pair), _F32),
                        pltpu.VMEM((2, QK_HEAD_DIM, GDN_V_HEAD_DIM), _F32),
                        pltpu.VMEM((n_sys, GDN_CHUNK, GDN_CHUNK), _F32),
                        pltpu.VMEM((n_sys, GDN_CHUNK, GDN_CHUNK), _BF16),
                        pltpu.VMEM((2, n_sys, QK_HEAD_DIM + GDN_CHUNK, GDN_CHUNK), _BF16),
                        pltpu.VMEM((2, n_sys, GDN_CHUNK, QK_HEAD_DIM), _F32),
                        pltpu.VMEM((2, n_sys, GDN_CHUNK, QK_HEAD_DIM + GDN_V_HEAD_DIM), _BF16),
                        pltpu.VMEM((2, n_sys, CONV_HALO, GDN_V_HEAD_DIM), _F32)],
        compiler_params=_cparams(("arbitrary",)),
        name="gated_delta",
    )(h, h, h, conv_groups, gates, norm_g.reshape(1, GDN_V_HEAD_DIM))


def _out_kernel(tok_ref, mq_ref, z0_ref, z1_ref, z2_ref, z3_ref, kv_ref, w_ref, x_ref, g_ref, b_ref,
                o_ref, ob_ref, y_s):
    tm, d = x_ref.shape
    kc = MIX_WIDTH // OUT_K_SPLIT
    z_refs = (z0_ref, z1_ref, z2_ref, z3_ref)

    @pl.when(pl.program_id(0) == 0)
    def _():
        y_s[...] = jnp.zeros_like(y_s)

    def attend(m):
        lo = m * MEM_HEAD_DIM
        mk = kv_ref[0, :, lo:lo + MEM_HEAD_DIM]
        mv = kv_ref[0, :, MEM_WIDTH + lo:MEM_WIDTH + lo + MEM_HEAD_DIM]
        sc = _dot_nt(mq_ref[:, lo:lo + MEM_HEAD_DIM], mk) * (MEM_HEAD_DIM ** -0.5)
        p = jnp.exp(sc - jnp.max(sc, -1, keepdims=True))
        p = p / jnp.sum(p, -1, keepdims=True)
        return _dot(p.astype(_BF16), mv)

    def branch_chunk(k):
        lo = k * kc
        z = z_refs[lo // PROJ_TILE][:, lo % PROJ_TILE:lo % PROJ_TILE + kc].astype(_F32)
        if lo < TOK_WIDTH:
            val = tok_ref[:, lo:lo + kc].astype(_F32)
        else:
            first = (lo - TOK_WIDTH) // MEM_HEAD_DIM
            val = jnp.concatenate([attend(first + j) for j in range(kc // MEM_HEAD_DIM)], 1)
        return (val * _silu(z)).astype(_BF16)

    def project_stage():
        acc = None
        for k in range(OUT_K_SPLIT):
            part = _dot(branch_chunk(k), w_ref[0, k * kc:(k + 1) * kc, :])
            acc = part if acc is None else acc + part
            yield
        y_s[...] = acc

    def norm_stage():
        rows = tm // OUT_K_SPLIT
        for blk in range(OUT_K_SPLIT):
            sl = slice(blk * rows, (blk + 1) * rows)
            r = DEEPNORM_ALPHA * x_ref[sl, :] + y_s[sl, :]
            mu = jnp.mean(r, -1, keepdims=True)
            dev = r - mu
            var = jnp.mean(dev * dev, -1, keepdims=True)
            out = dev * lax.rsqrt(var + LN_EPS) * g_ref[...] + b_ref[...]
            o_ref[sl, :] = out
            ob_ref[sl, :] = out.astype(_BF16)
            yield

    _interleave((norm_stage(), OUT_K_SPLIT), (project_stage(), OUT_K_SPLIT + 1))


def _gate_out_ln(tok, h, kv, w_out_b, layer, x, ln_g, ln_b, seq, tm=OUT_ROWS):
    m, d = x.shape
    tm = min(tm, seq)
    n_tiles = m // tm
    n_mem = kv.shape[1]
    z_blk0 = (CONV_CH + MEM_WIDTH) // PROJ_TILE
    once = pl.Buffered(1)

    def gated(i):
        return jnp.minimum(i, n_tiles - 1)

    def normed(i):
        return jnp.maximum(i - 1, 0)

    def h_cols(blk):
        return pl.BlockSpec((tm, PROJ_TILE), lambda i: (gated(i), blk))

    return pl.pallas_call(
        _out_kernel,
        grid=(n_tiles + 1,),
        in_specs=[pl.BlockSpec((tm, TOK_WIDTH), lambda i: (gated(i), 0)),
                  h_cols(CONV_CH // PROJ_TILE),
                  h_cols(z_blk0), h_cols(z_blk0 + 1), h_cols(z_blk0 + 2), h_cols(z_blk0 + 3),
                  pl.BlockSpec((1, n_mem, 2 * MEM_WIDTH), lambda i: (gated(i) * tm // seq, 0, layer)),
                  pl.BlockSpec((1, MIX_WIDTH, d), lambda i: (layer, 0, 0), pipeline_mode=once),
                  pl.BlockSpec((tm, d), lambda i: (normed(i), 0)),
                  pl.BlockSpec((1, d), lambda i: (0, 0)),
                  pl.BlockSpec((1, d), lambda i: (0, 0))],
        out_specs=[pl.BlockSpec((tm, d), lambda i: (normed(i), 0))] * 2,
        out_shape=[jax.ShapeDtypeStruct((m, d), _F32), jax.ShapeDtypeStruct((m, d), _BF16)],
        scratch_shapes=[pltpu.VMEM((tm, d), _F32)],
        compiler_params=_cparams(("arbitrary",)),
        name="gate_out_layernorm",
    )(tok, h, h, h, h, h, kv, w_out_b, x, ln_g.reshape(1, d), ln_b.reshape(1, d))


def kernel(x, mem, positions, w_in_ret, ret_norm_g, w_in_gdn, conv_w, a_log, dt_bias, gdn_norm_g,
           w_mem_kv, w_out, ln_g, ln_b):
    b, s, d = x.shape
    m = b * s
    n_mem = mem.shape[1]
    cos, sin = _rope_tables(positions)
    mem_b = mem.reshape(b * n_mem, d).astype(_BF16)
    w_out_b = w_out.astype(_BF16)
    w_in_gdn_t = jnp.swapaxes(w_in_gdn, 1, 2)
    kv = _project(mem_b, w_mem_kv, None, 2 * MEM_WIDTH).reshape(b, n_mem, DEPTH * 2 * MEM_WIDTH)
    xf = x.reshape(m, d)
    xb = xf.astype(_BF16)
    for i in range(DEPTH):
        j = i // 2
        if i % 2 == 0:
            h = _project(xb, w_in_ret, j, RET_COLS)
            tok = _retention(h.reshape(b, s, RET_COLS), cos, sin, ret_norm_g[j])
        else:
            h = _project(xb, w_in_gdn_t, j, RET_COLS, w_is_transposed=True)
            gates = _gdn_gates(xb.reshape(b, s, d), w_in_gdn[j, :, RET_COLS:], a_log[j], dt_bias[j])
            tok = _gdn(h.reshape(b, s, RET_COLS), conv_w[j], gates, gdn_norm_g[j])
        xf, xb = _gate_out_ln(tok.reshape(m, TOK_WIDTH), h, kv, w_out_b, i, xf, ln_g[i], ln_b[i], s)
    return xf.reshape(b, s, d)
```

```python
import functools

import numpy as np
import jax
import jax.numpy as jnp
from jax import lax
from jax.experimental import pallas as pl
from jax.experimental.pallas import tpu as pltpu

QK_HEAD_DIM = 128
N_QK_HEADS = 12
QK_WIDTH = N_QK_HEADS * QK_HEAD_DIM
TOK_WIDTH = 3072
RET_V_HEAD_DIM = 256
GDN_V_HEAD_DIM = 128
N_GDN_V_HEADS = 24
MEM_HEADS = 4
MEM_HEAD_DIM = 256
MEM_WIDTH = MEM_HEADS * MEM_HEAD_DIM
MIX_WIDTH = TOK_WIDTH + MEM_WIDTH
CONV_WIDTH = 4
CONV_CH = 2 * QK_WIDTH + TOK_WIDTH
RET_COLS = CONV_CH + MEM_WIDTH + MIX_WIDTH
ROPE_BASE = 10000.0
DEPTH = 4
DEEPNORM_ALPHA = (2.0 * DEPTH) ** 0.25
LN_EPS = 1e-5
NORM_EPS = 1e-6

V7X_VMEM_BYTES = 64 * 1024 * 1024
VMEM_LIMIT = V7X_VMEM_BYTES - 8 * 1024 * 1024
LANES = 128

PROJ_TILE = 1024
PROJ_ROWS = 2048
RET_CHUNK = 512
RET_HEADS_PER_STEP = 4
GDN_CHUNK = 128
GDN_ROWS = 512
CONV_HALO = 8
OUT_ROWS = 256
OUT_K_SPLIT = 4

_F32 = jnp.float32
_BF16 = jnp.bfloat16


def _cparams(sem):
    return pltpu.CompilerParams(dimension_semantics=sem, vmem_limit_bytes=VMEM_LIMIT)


def _dot(a, b):
    return jnp.dot(a, b, preferred_element_type=_F32)


def _dot_nt(a, b):
    return lax.dot_general(a, b, (((1,), (1,)), ((), ())), preferred_element_type=_F32)


def _dot_tn(a, b):
    return lax.dot_general(a, b, (((0,), (0,)), ((), ())), preferred_element_type=_F32)


def _silu(z):
    return z * jax.nn.sigmoid(z)


def _interleave(*stages):
    live = [[gen, count, 0] for gen, count in stages]
    while live:
        stage = min(live, key=lambda st: (st[2] + 1) / st[1])
        try:
            next(stage[0])
            stage[2] += 1
        except StopIteration:
            live.remove(stage)


def _proj_kernel(w_is_transposed, x_ref, w_ref, o_ref, wb_ref):
    @pl.when(pl.program_id(1) == 0)
    def _():
        w = w_ref[0]
        wb_ref[...] = (w.T if w_is_transposed else w).astype(_BF16)

    o_ref[...] = _dot(x_ref[...], wb_ref[...]).astype(o_ref.dtype)


def _project(x, w_stack, layer, n_cols, w_is_transposed=False, tile=PROJ_TILE, rows=PROJ_ROWS):
    m, k = x.shape
    tm = min(rows, m)
    assert m % tm == 0 and n_cols % tile == 0
    col_tiles = n_cols // tile
    n_layers = w_stack.shape[0] if layer is None else 1

    def which(j):
        return (j // col_tiles, j % col_tiles) if layer is None else (layer, j)

    if w_is_transposed:
        w_spec = pl.BlockSpec((1, tile, k), lambda j, i: (which(j)[0], which(j)[1], 0))
    else:
        w_spec = pl.BlockSpec((1, k, tile), lambda j, i: (which(j)[0], 0, which(j)[1]))
    return pl.pallas_call(
        functools.partial(_proj_kernel, w_is_transposed),
        grid=(n_layers * col_tiles, m // tm),
        in_specs=[pl.BlockSpec((tm, k), lambda j, i: (i, 0)), w_spec],
        out_specs=pl.BlockSpec((tm, tile), lambda j, i: (i, j)),
        out_shape=jax.ShapeDtypeStruct((m, n_layers * n_cols), _BF16),
        scratch_shapes=[pltpu.VMEM((k, tile), _BF16)],
        compiler_params=_cparams(("parallel", "arbitrary")),
        name="proj_matmul",
    )(x, w_stack)


def _rope_kernel(pos_ref, freq_ref, cos_ref, sin_ref):
    ang = pos_ref[0].astype(_F32) * freq_ref[...]
    lane = lax.broadcasted_iota(jnp.int32, ang.shape, 1)
    s = jnp.sin(ang)
    cos_ref[0] = jnp.cos(ang)
    sin_ref[0] = jnp.where(lane < QK_HEAD_DIM // 2, -s, s)


def _rope_tables(positions, ts=512):
    b, s = positions.shape
    half = QK_HEAD_DIM // 2
    inv_freq = (ROPE_BASE ** (-np.arange(half, dtype=np.float32) / np.float32(half))).astype(np.float32)
    freq = jnp.asarray(np.concatenate([inv_freq, inv_freq])[None, :])
    ts = min(ts, s)
    out = jax.ShapeDtypeStruct((b, s, QK_HEAD_DIM), _F32)
    return pl.pallas_call(
        _rope_kernel,
        grid=(b, s // ts),
        in_specs=[pl.BlockSpec((1, ts, 1), lambda i, j: (i, j, 0)),
                  pl.BlockSpec((1, QK_HEAD_DIM), lambda i, j: (0, 0))],
        out_specs=[pl.BlockSpec((1, ts, QK_HEAD_DIM), lambda i, j: (i, j, 0))] * 2,
        out_shape=[out, out],
        compiler_params=_cparams(("parallel", "parallel")),
        name="rope_tables",
    )(positions.reshape(b, s, 1), freq)


def _retention_kernel(lg_ref, q_ref, k_ref, v_ref, cos_ref, sin_ref, g_ref, o_ref, state_ref, decay_ref):
    c = q_ref.shape[1]
    dk, dv = QK_HEAD_DIM, RET_V_HEAD_DIM
    pair = pl.program_id(1)

    @pl.when(pl.program_id(2) == 0)
    def _():
        state_ref[...] = jnp.zeros_like(state_ref)
        ii = lax.broadcasted_iota(jnp.int32, (c, c), 0)
        jj = lax.broadcasted_iota(jnp.int32, (c, c), 1)
        rel = (ii - jj).astype(_F32)
        for j in range(RET_HEADS_PER_STEP):
            lg = lg_ref[RET_HEADS_PER_STEP * pair + j]
            decay_ref[j] = jnp.where(rel >= 0, jnp.exp(lg * jnp.maximum(rel, 0.0)), 0.0)

    cos = cos_ref[0]
    sin = sin_ref[0]
    idx = lax.broadcasted_iota(jnp.int32, (c, dk), 0).astype(_F32)

    def head(j):
        lg = lg_ref[RET_HEADS_PER_STEP * pair + j]
        q = q_ref[0, :, j * dk:(j + 1) * dk].astype(_F32)
        k = k_ref[0, :, j * dk:(j + 1) * dk].astype(_F32)
        qr = q * cos + pltpu.roll(q, dk // 2, 1) * sin
        kr = (k * cos + pltpu.roll(k, dk // 2, 1) * sin) * (dk ** -0.5)
        qb = qr.astype(_BF16)
        kb = kr.astype(_BF16)
        q_in = (qr * jnp.exp((idx + 1.0) * lg)).astype(_BF16)
        k_out = (kr * jnp.exp((c - 1.0 - idx) * lg)).astype(_BF16)
        yield
        scores = (_dot_nt(qb, kb) * decay_ref[j]).astype(_BF16)
        yield
        vb = v_ref[0, :, j * dv:(j + 1) * dv]
        state = state_ref[j]
        o = _dot(jnp.concatenate([scores, q_in], 1), jnp.concatenate([vb, state.astype(_BF16)], 0))
        state_ref[j] = state * jnp.exp(c * lg) + _dot_tn(k_out, vb)
        yield
        mu = jnp.mean(o, -1, keepdims=True)
        d = o - mu
        var = jnp.mean(d * d, -1, keepdims=True)
        o_ref[0, :, j * dv:(j + 1) * dv] = (
            d * lax.rsqrt(var + NORM_EPS) * g_ref[:, j * dv:(j + 1) * dv]).astype(o_ref.dtype)

    _interleave(*[(head(j), 4) for j in range(RET_HEADS_PER_STEP)])


def _retention(h, cos, sin, norm_g, c=RET_CHUNK):
    b, s, _ = h.shape
    c = min(c, s)
    hp = RET_HEADS_PER_STEP
    log_gamma = jnp.asarray(np.log1p(-np.exp2(-5.0 - np.arange(N_QK_HEADS, dtype=np.float64))).astype(np.float32))
    n_pairs = N_QK_HEADS // hp
    v_blk0 = 2 * QK_WIDTH // (hp * RET_V_HEAD_DIM)
    grid_spec = pltpu.PrefetchScalarGridSpec(
        num_scalar_prefetch=1,
        grid=(b, n_pairs, s // c),
        in_specs=[
            pl.BlockSpec((1, c, hp * QK_HEAD_DIM), lambda i, hd, t, lg: (i, t, hd)),
            pl.BlockSpec((1, c, hp * QK_HEAD_DIM), lambda i, hd, t, lg: (i, t, n_pairs + hd)),
            pl.BlockSpec((1, c, hp * RET_V_HEAD_DIM), lambda i, hd, t, lg: (i, t, v_blk0 + hd)),
            pl.BlockSpec((1, c, QK_HEAD_DIM), lambda i, hd, t, lg: (i, t, 0)),
            pl.BlockSpec((1, c, QK_HEAD_DIM), lambda i, hd, t, lg: (i, t, 0)),
            pl.BlockSpec((1, hp * RET_V_HEAD_DIM), lambda i, hd, t, lg: (0, hd)),
        ],
        out_specs=pl.BlockSpec((1, c, hp * RET_V_HEAD_DIM), lambda i, hd, t, lg: (i, t, hd)),
        scratch_shapes=[pltpu.VMEM((hp, QK_HEAD_DIM, RET_V_HEAD_DIM), _F32),
                        pltpu.VMEM((hp, c, c), _F32)],
    )
    return pl.pallas_call(
        _retention_kernel,
        grid_spec=grid_spec,
        out_shape=jax.ShapeDtypeStruct((b, s, TOK_WIDTH), _BF16),
        compiler_params=_cparams(("parallel", "parallel", "arbitrary")),
        name="retention",
    )(log_gamma, h, h, h, cos, sin, norm_g.reshape(1, TOK_WIDTH))


def _gdn_gate_kernel(x_ref, w_ref, alog_ref, dtb_ref, gb_ref):
    ab = _dot_nt(w_ref[...].T.astype(_BF16), x_ref[0])
    a = ab[:N_GDN_V_HEADS]
    bb = ab[N_GDN_V_HEADS:2 * N_GDN_V_HEADS]
    g = -jnp.exp(alog_ref[...]) * jax.nn.softplus(a + dtb_ref[...])
    lane = lax.broadcasted_iota(jnp.int32, g.shape, 1) % GDN_CHUNK
    shift = 1
    while shift < GDN_CHUNK:
        g = g + jnp.where(lane >= shift, pltpu.roll(g, shift, 1), 0.0)
        shift *= 2
    gb_ref[0, 0] = g
    gb_ref[0, 1] = jax.nn.sigmoid(bb)


def _gdn_gates(xb, w_ab, a_log, dt_bias, ts=2048):
    b, s, d = xb.shape
    ts = min(ts, s)
    w_ab = jnp.pad(w_ab, ((0, 0), (0, LANES - w_ab.shape[1])))
    return pl.pallas_call(
        _gdn_gate_kernel,
        grid=(b, s // ts),
        in_specs=[pl.BlockSpec((1, ts, d), lambda i, j: (i, j, 0)),
                  pl.BlockSpec((d, LANES), lambda i, j: (0, 0)),
                  pl.BlockSpec((N_GDN_V_HEADS, 1), lambda i, j: (0, 0)),
                  pl.BlockSpec((N_GDN_V_HEADS, 1), lambda i, j: (0, 0))],
        out_specs=pl.BlockSpec((1, 2, N_GDN_V_HEADS, ts), lambda i, j: (i, 0, 0, j)),
        out_shape=jax.ShapeDtypeStruct((b, 2, N_GDN_V_HEADS, s), _F32),
        compiler_params=_cparams(("parallel", "parallel")),
        name="gdn_gates",
    )(xb, w_ab, a_log.reshape(-1, 1).astype(_F32), dt_bias.reshape(-1, 1).astype(_F32))


def _unit_lower_inverses(lows, ii, jj):
    n = lows[0].shape[0]
    xor = ii ^ jj
    eye = (ii == jj).astype(_F32)
    d1s = [jnp.where(xor < 8, low, 0.0) for low in lows]
    d1bs = [d1.astype(_BF16) for d1 in d1s]
    d2bs = [_dot(d1b, d1b).astype(_BF16) for d1b in d1bs]
    yield None
    ts = [eye - d1 for d1 in d1s]
    ts = [t + _dot(t.astype(_BF16), d2b) for t, d2b in zip(ts, d2bs)]
    yield None
    d4bs = [_dot(d2b, d2b).astype(_BF16) for d2b in d2bs]
    yield None
    ts = [t + _dot(t.astype(_BF16), d4b) for t, d4b in zip(ts, d4bs)]
    yield None
    neg_lows = [-low.astype(_BF16) for low in lows]
    tbs = [t.astype(_BF16) for t in ts]
    size = 8
    while size < n:
        band = (xor >= size) & (xor < 2 * size)
        ys = [_dot(tb, nl).astype(_BF16) for tb, nl in zip(tbs, neg_lows)]
        yield None
        tbs = [jnp.where(band, _dot(y, tb).astype(_BF16), tb) for y, tb in zip(ys, tbs)]
        yield None
        size *= 2
    yield tbs


def _conv_silu_rows(buf_ref, w, r0, n):
    y = None
    for lag in range(CONV_WIDTH):
        lo = CONV_HALO + r0 - lag
        term = buf_ref[lo:lo + n, :] * w[CONV_WIDTH - 1 - lag:CONV_WIDTH - lag, :]
        y = term if y is None else y + term
    return _silu(y)


def _gdn_kernel(blocks_per_seq, n_blocks, q_ref, k_ref, v_ref, cw_ref, gb_ref, ng_ref, o_ref,
                qbuf, kbuf, vbuf, state_ref, low_s, t_s, kp_s, qg_s, rhs_s, gl_s):
    rows = q_ref.shape[1]
    n = GDN_CHUNK
    dv = GDN_V_HEAD_DIM
    n_sys = 2 * (rows // n)
    g_step = pl.program_id(0)
    slot = g_step % 2

    @pl.when(g_step == 0)
    def _():
        for ref in (low_s, t_s, kp_s, qg_s, rhs_s, gl_s, state_ref):
            ref[...] = jnp.zeros_like(ref)

    @pl.when(g_step % blocks_per_seq == 0)
    def _():
        for ref in (qbuf, kbuf, vbuf):
            ref[0:CONV_HALO, :] = jnp.zeros((CONV_HALO, ref.shape[1]), _F32)

    ii = lax.broadcasted_iota(jnp.int32, (n, n), 0)
    jj = lax.broadcasted_iota(jnp.int32, (n, n), 1)
    causal = ii >= jj
    strict = ii > jj

    def recurrence_stage():
        fresh = (g_step - 2) % blocks_per_seq == 0
        ng = ng_ref[...]
        prods = []
        for idx in range(n_sys):
            wu = _dot(t_s[idx], rhs_s[slot, idx]).astype(_BF16)
            both = _dot(kp_s[slot, idx], wu)
            q_eff = (qg_s[slot, idx] - both[n:, :n]).astype(_BF16)
            prods.append((jnp.concatenate([q_eff, both[:n, :n].astype(_BF16)], 0), both[:n, n:],
                          both[n:, n:], gl_s[slot, idx, 0:1, :]))
            yield
        states = [jnp.where(fresh, 0.0, state_ref[e]) for e in range(2)]
        for idx, (qa, n_mat, pu, gl) in enumerate(prods):
            c, e = divmod(idx, 2)
            qa_s = _dot(qa, states[e].astype(_BF16))
            out = qa_s[:n] + pu
            states[e] = states[e] * gl - qa_s[n:] + n_mat
            out = out * lax.rsqrt(jnp.mean(out * out, -1, keepdims=True) + NORM_EPS) * ng
            o_ref[0, c * n:(c + 1) * n, e * dv:(e + 1) * dv] = out.astype(o_ref.dtype)
            yield
        state_ref[0] = states[0]
        state_ref[1] = states[1]

    def inverse_stage():
        for t_invs in _unit_lower_inverses([low_s[idx] for idx in range(n_sys)], ii, jj):
            yield
        for idx, t_inv in enumerate(t_invs):
            t_s[idx] = t_inv

    def operand_stage():
        bufs = ((qbuf, q_ref), (kbuf, k_ref), (vbuf, v_ref))
        for buf, ref in bufs:
            buf[CONV_HALO:CONV_HALO + rows, :] = ref[0].astype(_F32)
        yield
        head = (jnp.minimum(g_step, n_blocks - 1) // blocks_per_seq) % N_QK_HEADS
        wq = cw_ref[head]
        wk = cw_ref[N_QK_HEADS + head]
        wv = jnp.concatenate([cw_ref[2 * N_QK_HEADS + 2 * head], cw_ref[2 * N_QK_HEADS + 2 * head + 1]], 1)
        for c in range(rows // n):
            r0 = c * n
            qc = _conv_silu_rows(qbuf, wq, r0, n)
            kc = _conv_silu_rows(kbuf, wk, r0, n)
            qn = qc * (lax.rsqrt(jnp.sum(qc * qc, -1, keepdims=True) + NORM_EPS) * (QK_HEAD_DIM ** -0.5))
            kn = kc * lax.rsqrt(jnp.sum(kc * kc, -1, keepdims=True) + NORM_EPS)
            kb = kn.astype(_BF16)
            kq = _dot_nt(jnp.concatenate([kb, qn.astype(_BF16)], 0), kb)
            kk, qk = kq[:n], kq[n:]
            yield
            vc = _conv_silu_rows(vbuf, wv, r0, n)
            for e in range(2):
                idx = 2 * c + e
                g_row = jnp.broadcast_to(gb_ref[0, 0, 0, e:e + 1, r0:r0 + n], (n, n))
                g_col = g_row.T
                b_col = jnp.broadcast_to(gb_ref[0, 1, 0, e:e + 1, r0:r0 + n], (n, n)).T
                decay = jnp.where(causal, jnp.exp(g_col - g_row), 0.0)
                e_g = jnp.exp(g_col)
                g_last = g_col[n - 1:n, :]
                v_e = vc[:, e * dv:(e + 1) * dv]
                low_s[idx] = jnp.where(strict, b_col * kk * decay, 0.0)
                k_tail_t = (kn * jnp.exp(g_last - g_col)).T
                kp_s[slot, idx] = jnp.concatenate([k_tail_t, qk * decay], 0).astype(_BF16)
                qg_s[slot, idx] = qn * e_g
                rhs_s[slot, idx] = jnp.concatenate([b_col * e_g * kn, b_col * v_e], 1).astype(_BF16)
                gl_s[slot, idx] = jnp.broadcast_to(jnp.exp(g_last), (CONV_HALO, dv))
                yield
        for buf, _ in bufs:
            buf[0:CONV_HALO, :] = buf[rows:rows + CONV_HALO, :]

    n_chunks = rows // n
    _interleave((inverse_stage(), 12), (operand_stage(), 3 * n_chunks + 1), (recurrence_stage(), 2 * n_sys + 1))


def _gdn(h, conv_w, gates, norm_g, rows=GDN_ROWS):
    b, s, _ = h.shape
    rows = min(rows, s)
    nb = s // rows
    total = b * N_QK_HEADS * nb
    v_blk0 = 2 * QK_WIDTH // (2 * GDN_V_HEAD_DIM)
    gates = gates.reshape(b, 2, N_QK_HEADS, 2, s)
    conv_groups = conv_w.reshape(CONV_WIDTH, CONV_CH // LANES, LANES).transpose(1, 0, 2)
    pair = 2 * GDN_V_HEAD_DIM
    n_sys = 2 * (rows // GDN_CHUNK)

    def where(g):
        g = jnp.minimum(g, total - 1)
        return g // (N_QK_HEADS * nb), (g // nb) % N_QK_HEADS, g % nb

    def cur(col0):
        def index(g):
            i, hd, t = where(g)
            return i, t, col0 + hd
        return index

    def gate_index(g):
        i, hd, t = where(g)
        return i, 0, hd, 0, t

    def out_index(g):
        i, hd, t = where(jnp.maximum(g - 2, 0))
        return i, t, hd

    return pl.pallas_call(
        functools.partial(_gdn_kernel, nb, total),
        grid=(total + 2,),
        in_specs=[
            pl.BlockSpec((1, rows, QK_HEAD_DIM), cur(0)),
            pl.BlockSpec((1, rows, QK_HEAD_DIM), cur(N_QK_HEADS)),
            pl.BlockSpec((1, rows, pair), cur(v_blk0)),
            pl.BlockSpec((CONV_CH // LANES, CONV_WIDTH, LANES), lambda g: (0, 0, 0)),
            pl.BlockSpec((1, 2, 1, 2, rows), gate_index),
            pl.BlockSpec((1, GDN_V_HEAD_DIM), lambda g: (0, 0)),
        ],
        out_specs=pl.BlockSpec((1, rows, pair), out_index),
        out_shape=jax.ShapeDtypeStruct((b, s, TOK_WIDTH), _BF16),
        scratch_shapes=[pltpu.VMEM((rows + CONV_HALO, QK_HEAD_DIM), _F32),
                        pltpu.VMEM((rows + CONV_HALO, QK_HEAD_DIM), _F32),
                        pltpu.VMEM((rows + CONV_HALO, pair), _F32),
                        pltpu.VMEM((2, QK_HEAD_DIM, GDN_V_HEAD_DIM), _F32),
                        pltpu.VMEM((n_sys, GDN_CHUNK, GDN_CHUNK), _F32),
                        pltpu.VMEM((n_sys, GDN_CHUNK, GDN_CHUNK), _BF16),
                        pltpu.VMEM((2, n_sys, QK_HEAD_DIM + GDN_CHUNK, GDN_CHUNK), _BF16),
                        pltpu.VMEM((2, n_sys, GDN_CHUNK, QK_HEAD_DIM), _F32),
                        pltpu.VMEM((2, n_sys, GDN_CHUNK, QK_HEAD_DIM + GDN_V_HEAD_DIM), _BF16),
                        pltpu.VMEM((2, n_sys, CONV_HALO, GDN_V_HEAD_DIM), _F32)],
        compiler_params=_cparams(("arbitrary",)),
        name="gated_delta",
    )(h, h, h, conv_groups, gates, norm_g.reshape(1, GDN_V_HEAD_DIM))


def _out_kernel(tok_ref, mq_ref, z0_ref, z1_ref, z2_ref, z3_ref, kv_ref, w_ref, x_ref, g_ref, b_ref,
                o_ref, ob_ref, y_s):
    tm, d = x_ref.shape
    kc = MIX_WIDTH // OUT_K_SPLIT
    z_refs = (z0_ref, z1_ref, z2_ref, z3_ref)

    @pl.when(pl.program_id(0) == 0)
    def _():
        y_s[...] = jnp.zeros_like(y_s)

    def attend(m):
        lo = m * MEM_HEAD_DIM
        mk = kv_ref[0, :, lo:lo + MEM_HEAD_DIM]
        mv = kv_ref[0, :, MEM_WIDTH + lo:MEM_WIDTH + lo + MEM_HEAD_DIM]
        sc = _dot_nt(mq_ref[:, lo:lo + MEM_HEAD_DIM], mk) * (MEM_HEAD_DIM ** -0.5)
        p = jnp.exp(sc - jnp.max(sc, -1, keepdims=True))
        p = p / jnp.sum(p, -1, keepdims=True)
        return _dot(p.astype(_BF16), mv)

    def branch_chunk(k):
        lo = k * kc
        z = z_refs[lo // PROJ_TILE][:, lo % PROJ_TILE:lo % PROJ_TILE + kc].astype(_F32)
        if lo < TOK_WIDTH:
            val = tok_ref[:, lo:lo + kc].astype(_F32)
        else:
            first = (lo - TOK_WIDTH) // MEM_HEAD_DIM
            val = jnp.concatenate([attend(first + j) for j in range(kc // MEM_HEAD_DIM)], 1)
        return (val * _silu(z)).astype(_BF16)

    def project_stage():
        acc = None
        for k in range(OUT_K_SPLIT):
            part = _dot(branch_chunk(k), w_ref[0, k * kc:(k + 1) * kc, :])
            acc = part if acc is None else acc + part
            yield
        y_s[...] = acc

    def norm_stage():
        rows = tm // OUT_K_SPLIT
        for blk in range(OUT_K_SPLIT):
            sl = slice(blk * rows, (blk + 1) * rows)
            r = DEEPNORM_ALPHA * x_ref[sl, :] + y_s[sl, :]
            mu = jnp.mean(r, -1, keepdims=True)
            dev = r - mu
            var = jnp.mean(dev * dev, -1, keepdims=True)
            out = dev * lax.rsqrt(var + LN_EPS) * g_ref[...] + b_ref[...]
            o_ref[sl, :] = out
            ob_ref[sl, :] = out.astype(_BF16)
            yield

    _interleave((norm_stage(), OUT_K_SPLIT), (project_stage(), OUT_K_SPLIT + 1))


def _gate_out_ln(tok, h, kv, w_out_b, layer, x, ln_g, ln_b, seq, tm=OUT_ROWS):
    m, d = x.shape
    tm = min(tm, seq)
    n_tiles = m // tm
    n_mem = kv.shape[1]
    z_blk0 = (CONV_CH + MEM_WIDTH) // PROJ_TILE
    once = pl.Buffered(1)

    def gated(i):
        return jnp.minimum(i, n_tiles - 1)

    def normed(i):
        return jnp.maximum(i - 1, 0)

    def h_cols(blk):
        return pl.BlockSpec((tm, PROJ_TILE), lambda i: (gated(i), blk))

    return pl.pallas_call(
        _out_kernel,
        grid=(n_tiles + 1,),
        in_specs=[pl.BlockSpec((tm, TOK_WIDTH), lambda i: (gated(i), 0)),
                  h_cols(CONV_CH // PROJ_TILE),
                  h_cols(z_blk0), h_cols(z_blk0 + 1), h_cols(z_blk0 + 2), h_cols(z_blk0 + 3),
                  pl.BlockSpec((1, n_mem, 2 * MEM_WIDTH), lambda i: (gated(i) * tm // seq, 0, layer)),
                  pl.BlockSpec((1, MIX_WIDTH, d), lambda i: (layer, 0, 0), pipeline_mode=once),
                  pl.BlockSpec((tm, d), lambda i: (normed(i), 0)),
                  pl.BlockSpec((1, d), lambda i: (0, 0)),
                  pl.BlockSpec((1, d), lambda i: (0, 0))],
        out_specs=[pl.BlockSpec((tm, d), lambda i: (normed(i), 0))] * 2,
        out_shape=[jax.ShapeDtypeStruct((m, d), _F32), jax.ShapeDtypeStruct((m, d), _BF16)],
        scratch_shapes=[pltpu.VMEM((tm, d), _F32)],
        compiler_params=_cparams(("arbitrary",)),
        name="gate_out_layernorm",
    )(tok, h, h, h, h, h, kv, w_out_b, x, ln_g.reshape(1, d), ln_b.reshape(1, d))


def kernel(x, mem, positions, w_in_ret, ret_norm_g, w_in_gdn, conv_w, a_log, dt_bias, gdn_norm_g,
           w_mem_kv, w_out, ln_g, ln_b):
    b, s, d = x.shape
    m = b * s
    n_mem = mem.shape[1]
    cos, sin = _rope_tables(positions)
    mem_b = mem.reshape(b * n_mem, d).astype(_BF16)
    w_out_b = w_out.astype(_BF16)
    w_in_gdn_t = jnp.swapaxes(w_in_gdn, 1, 2)
    kv = _project(mem_b, w_mem_kv, None, 2 * MEM_WIDTH).reshape(b, n_mem, DEPTH * 2 * MEM_WIDTH)
    xf = x.reshape(m, d)
    xb = xf.astype(_BF16)
    for i in range(DEPTH):
        j = i // 2
        if i % 2 == 0:
            h = _project(xb, w_in_ret, j, RET_COLS)
            tok = _retention(h.reshape(b, s, RET_COLS), cos, sin, ret_norm_g[j])
        else:
            h = _project(xb, w_in_gdn_t, j, RET_COLS, w_is_transposed=True)
            gates = _gdn_gates(xb.reshape(b, s, d), w_in_gdn[j, :, RET_COLS:], a_log[j], dt_bias[j])
            tok = _gdn(h.reshape(b, s, RET_COLS), conv_w[j], gates, gdn_norm_g[j])
        xf, xb = _gate_out_ln(tok.reshape(m, TOK_WIDTH), h, kv, w_out_b, i, xf, ln_g[i], ln_b[i], s)
    return xf.reshape(b, s, d)
```

```python
import functools

import numpy as np
import jax
import jax.numpy as jnp
from jax import lax
from jax.experimental import pallas as pl
from jax.experimental.pallas import tpu as pltpu

QK_HEAD_DIM = 128
N_QK_HEADS = 12
QK_WIDTH = N_QK_HEADS * QK_HEAD_DIM
TOK_WIDTH = 3072
RET_V_HEAD_DIM = 256
GDN_V_HEAD_DIM = 128
N_GDN_V_HEADS = 24
MEM_HEADS = 4
MEM_HEAD_DIM = 256
MEM_WIDTH = MEM_HEADS * MEM_HEAD_DIM
MIX_WIDTH = TOK_WIDTH + MEM_WIDTH
CONV_WIDTH = 4
CONV_CH = 2 * QK_WIDTH + TOK_WIDTH
RET_COLS = CONV_CH + MEM_WIDTH + MIX_WIDTH
ROPE_BASE = 10000.0
DEPTH = 4
DEEPNORM_ALPHA = (2.0 * DEPTH) ** 0.25
LN_EPS = 1e-5
NORM_EPS = 1e-6

V7X_VMEM_BYTES = 64 * 1024 * 1024
VMEM_LIMIT = V7X_VMEM_BYTES - 8 * 1024 * 1024
LANES = 128

PROJ_TILE = 1024
PROJ_ROWS = 2048
RET_CHUNK = 512
RET_HEADS_PER_STEP = 4
GDN_CHUNK = 128
GDN_ROWS = 512
CONV_HALO = 8
OUT_ROWS = 256
OUT_K_SPLIT = 4

_F32 = jnp.float32
_BF16 = jnp.bfloat16


def _cparams(sem):
    return pltpu.CompilerParams(dimension_semantics=sem, vmem_limit_bytes=VMEM_LIMIT)


def _dot(a, b):
    return jnp.dot(a, b, preferred_element_type=_F32)


def _dot_nt(a, b):
    return lax.dot_general(a, b, (((1,), (1,)), ((), ())), preferred_element_type=_F32)


def _dot_tn(a, b):
    return lax.dot_general(a, b, (((0,), (0,)), ((), ())), preferred_element_type=_F32)


def _silu(z):
    return z * jax.nn.sigmoid(z)


def _interleave(*stages):
    live = [[gen, count, 0] for gen, count in stages]
    while live:
        stage = min(live, key=lambda st: (st[2] + 1) / st[1])
        try:
            next(stage[0])
            stage[2] += 1
        except StopIteration:
            live.remove(stage)


def _proj_kernel(w_is_transposed, x_ref, w_ref, o_ref, wb_ref):
    @pl.when(pl.program_id(1) == 0)
    def _():
        w = w_ref[0]
        wb_ref[...] = (w.T if w_is_transposed else w).astype(_BF16)

    o_ref[...] = _dot(x_ref[...], wb_ref[...]).astype(o_ref.dtype)


def _project(x, w_stack, layer, n_cols, w_is_transposed=False, tile=PROJ_TILE, rows=PROJ_ROWS):
    m, k = x.shape
    tm = min(rows, m)
    assert m % tm == 0 and n_cols % tile == 0
    col_tiles = n_cols // tile
    n_layers = w_stack.shape[0] if layer is None else 1

    def which(j):
        return (j // col_tiles, j % col_tiles) if layer is None else (layer, j)

    if w_is_transposed:
        w_spec = pl.BlockSpec((1, tile, k), lambda j, i: (which(j)[0], which(j)[1], 0))
    else:
        w_spec = pl.BlockSpec((1, k, tile), lambda j, i: (which(j)[0], 0, which(j)[1]))
    return pl.pallas_call(
        functools.partial(_proj_kernel, w_is_transposed),
        grid=(n_layers * col_tiles, m // tm),
        in_specs=[pl.BlockSpec((tm, k), lambda j, i: (i, 0)), w_spec],
        out_specs=pl.BlockSpec((tm, tile), lambda j, i: (i, j)),
        out_shape=jax.ShapeDtypeStruct((m, n_layers * n_cols), _BF16),
        scratch_shapes=[pltpu.VMEM((k, tile), _BF16)],
        compiler_params=_cparams(("parallel", "arbitrary")),
        name="proj_matmul",
    )(x, w_stack)


def _rope_kernel(pos_ref, freq_ref, cos_ref, sin_ref):
    ang = pos_ref[0].astype(_F32) * freq_ref[...]
    lane = lax.broadcasted_iota(jnp.int32, ang.shape, 1)
    s = jnp.sin(ang)
    cos_ref[0] = jnp.cos(ang)
    sin_ref[0] = jnp.where(lane < QK_HEAD_DIM // 2, -s, s)


def _rope_tables(positions, ts=512):
    b, s = positions.shape
    half = QK_HEAD_DIM // 2
    inv_freq = (ROPE_BASE ** (-np.arange(half, dtype=np.float32) / np.float32(half))).astype(np.float32)
    freq = jnp.asarray(np.concatenate([inv_freq, inv_freq])[None, :])
    ts = min(ts, s)
    out = jax.ShapeDtypeStruct((b, s, QK_HEAD_DIM), _F32)
    return pl.pallas_call(
        _rope_kernel,
        grid=(b, s // ts),
        in_specs=[pl.BlockSpec((1, ts, 1), lambda i, j: (i, j, 0)),
                  pl.BlockSpec((1, QK_HEAD_DIM), lambda i, j: (0, 0))],
        out_specs=[pl.BlockSpec((1, ts, QK_HEAD_DIM), lambda i, j: (i, j, 0))] * 2,
        out_shape=[out, out],
        compiler_params=_cparams(("parallel", "parallel")),
        name="rope_tables",
    )(positions.reshape(b, s, 1), freq)


def _retention_kernel(lg_ref, q_ref, k_ref, v_ref, cos_ref, sin_ref, g_ref, o_ref, state_ref, decay_ref):
    c = q_ref.shape[1]
    dk, dv = QK_HEAD_DIM, RET_V_HEAD_DIM
    pair = pl.program_id(1)

    @pl.when(pl.program_id(2) == 0)
    def _():
        state_ref[...] = jnp.zeros_like(state_ref)
        ii = lax.broadcasted_iota(jnp.int32, (c, c), 0)
        jj = lax.broadcasted_iota(jnp.int32, (c, c), 1)
        rel = (ii - jj).astype(_F32)
        for j in range(RET_HEADS_PER_STEP):
            lg = lg_ref[RET_HEADS_PER_STEP * pair + j]
            decay_ref[j] = jnp.where(rel >= 0, jnp.exp(lg * jnp.maximum(rel, 0.0)), 0.0)

    cos = cos_ref[0]
    sin = sin_ref[0]
    idx = lax.broadcasted_iota(jnp.int32, (c, dk), 0).astype(_F32)

    def head(j):
        lg = lg_ref[RET_HEADS_PER_STEP * pair + j]
        q = q_ref[0, :, j * dk:(j + 1) * dk].astype(_F32)
        k = k_ref[0, :, j * dk:(j + 1) * dk].astype(_F32)
        qr = q * cos + pltpu.roll(q, dk // 2, 1) * sin
        kr = (k * cos + pltpu.roll(k, dk // 2, 1) * sin) * (dk ** -0.5)
        qb = qr.astype(_BF16)
        kb = kr.astype(_BF16)
        q_in = (qr * jnp.exp((idx + 1.0) * lg)).astype(_BF16)
        k_out = (kr * jnp.exp((c - 1.0 - idx) * lg)).astype(_BF16)
        yield
        scores = (_dot_nt(qb, kb) * decay_ref[j]).astype(_BF16)
        yield
        vb = v_ref[0, :, j * dv:(j + 1) * dv]
        state = state_ref[j]
        o = _dot(jnp.concatenate([scores, q_in], 1), jnp.concatenate([vb, state.astype(_BF16)], 0))
        state_ref[j] = state * jnp.exp(c * lg) + _dot_tn(k_out, vb)
        yield
        mu = jnp.mean(o, -1, keepdims=True)
        d = o - mu
        var = jnp.mean(d * d, -1, keepdims=True)
        o_ref[0, :, j * dv:(j + 1) * dv] = (
            d * lax.rsqrt(var + NORM_EPS) * g_ref[:, j * dv:(j + 1) * dv]).astype(o_ref.dtype)

    _interleave(*[(head(j), 4) for j in range(RET_HEADS_PER_STEP)])


def _retention(h, cos, sin, norm_g, c=RET_CHUNK):
    b, s, _ = h.shape
    c = min(c, s)
    hp = RET_HEADS_PER_STEP
    log_gamma = jnp.asarray(np.log1p(-np.exp2(-5.0 - np.arange(N_QK_HEADS, dtype=np.float64))).astype(np.float32))
    n_pairs = N_QK_HEADS // hp
    v_blk0 = 2 * QK_WIDTH // (hp * RET_V_HEAD_DIM)
    grid_spec = pltpu.PrefetchScalarGridSpec(
        num_scalar_prefetch=1,
        grid=(b, n_pairs, s // c),
        in_specs=[
            pl.BlockSpec((1, c, hp * QK_HEAD_DIM), lambda i, hd, t, lg: (i, t, hd)),
            pl.BlockSpec((1, c, hp * QK_HEAD_DIM), lambda i, hd, t, lg: (i, t, n_pairs + hd)),
            pl.BlockSpec((1, c, hp * RET_V_HEAD_DIM), lambda i, hd, t, lg: (i, t, v_blk0 + hd)),
            pl.BlockSpec((1, c, QK_HEAD_DIM), lambda i, hd, t, lg: (i, t, 0)),
            pl.BlockSpec((1, c, QK_HEAD_DIM), lambda i, hd, t, lg: (i, t, 0)),
            pl.BlockSpec((1, hp * RET_V_HEAD_DIM), lambda i, hd, t, lg: (0, hd)),
        ],
        out_specs=pl.BlockSpec((1, c, hp * RET_V_HEAD_DIM), lambda i, hd, t, lg: (i, t, hd)),
        scratch_shapes=[pltpu.VMEM((hp, QK_HEAD_DIM, RET_V_HEAD_DIM), _F32),
                        pltpu.VMEM((hp, c, c), _F32)],
    )
    return pl.pallas_call(
        _retention_kernel,
        grid_spec=grid_spec,
        out_shape=jax.ShapeDtypeStruct((b, s, TOK_WIDTH), _BF16),
        compiler_params=_cparams(("parallel", "parallel", "arbitrary")),
        name="retention",
    )(log_gamma, h, h, h, cos, sin, norm_g.reshape(1, TOK_WIDTH))


def _gdn_gate_kernel(x_ref, w_ref, alog_ref, dtb_ref, gb_ref):
    ab = _dot_nt(w_ref[...].T.astype(_BF16), x_ref[0])
    a = ab[:N_GDN_V_HEADS]
    bb = ab[N_GDN_V_HEADS:2 * N_GDN_V_HEADS]
    g = -jnp.exp(alog_ref[...]) * jax.nn.softplus(a + dtb_ref[...])
    lane = lax.broadcasted_iota(jnp.int32, g.shape, 1) % GDN_CHUNK
    shift = 1
    while shift < GDN_CHUNK:
        g = g + jnp.where(lane >= shift, pltpu.roll(g, shift, 1), 0.0)
        shift *= 2
    gb_ref[0, 0] = g
    gb_ref[0, 1] = jax.nn.sigmoid(bb)


def _gdn_gates(xb, w_ab, a_log, dt_bias, ts=2048):
    b, s, d = xb.shape
    ts = min(ts, s)
    w_ab = jnp.pad(w_ab, ((0, 0), (0, LANES - w_ab.shape[1])))
    return pl.pallas_call(
        _gdn_gate_kernel,
        grid=(b, s // ts),
        in_specs=[pl.BlockSpec((1, ts, d), lambda i, j: (i, j, 0)),
                  pl.BlockSpec((d, LANES), lambda i, j: (0, 0)),
                  pl.BlockSpec((N_GDN_V_HEADS, 1), lambda i, j: (0, 0)),
                  pl.BlockSpec((N_GDN_V_HEADS, 1), lambda i, j: (0, 0))],
        out_specs=pl.BlockSpec((1, 2, N_GDN_V_HEADS, ts), lambda i, j: (i, 0, 0, j)),
        out_shape=jax.ShapeDtypeStruct((b, 2, N_GDN_V_HEADS, s), _F32),
        compiler_params=_cparams(("parallel", "parallel")),
        name="gdn_gates",
    )(xb, w_ab, a_log.reshape(-1, 1).astype(_F32), dt_bias.reshape(-1, 1).astype(_F32))


def _unit_lower_inverses(lows, ii, jj):
    n = lows[0].shape[0]
    xor = ii ^ jj
    eye = (ii == jj).astype(_F32)
    d1s = [jnp.where(xor < 8, low, 0.0) for low in lows]
    d1bs = [d1.astype(_BF16) for d1 in d1s]
    d2bs = [_dot(d1b, d1b).astype(_BF16) for d1b in d1bs]
    yield None
    ts = [eye - d1 for d1 in d1s]
    ts = [t + _dot(t.astype(_BF16), d2b) for t, d2b in zip(ts, d2bs)]
    yield None
    d4bs = [_dot(d2b, d2b).astype(_BF16) for d2b in d2bs]
    yield None
    ts = [t + _dot(t.astype(_BF16), d4b) for t, d4b in zip(ts, d4bs)]
    yield None
    neg_lows = [-low.astype(_BF16) for low in lows]
    tbs = [t.astype(_BF16) for t in ts]
    size = 8
    while size < n:
        band = (xor >= size) & (xor < 2 * size)
        ys = [_dot(tb, nl).astype(_BF16) for tb, nl in zip(tbs, neg_lows)]
        yield None
        tbs = [jnp.where(band, _dot(y, tb).astype(_BF16), tb) for y, tb in zip(ys, tbs)]
        yield None
        size *= 2
    yield tbs


def _conv_silu_rows(buf_ref, w, r0, n):
    y = None
    for lag in range(CONV_WIDTH):
        lo = CONV_HALO + r0 - lag
        term = buf_ref[lo:lo + n, :] * w[CONV_WIDTH - 1 - lag:CONV_WIDTH - lag, :]
        y = term if y is None else y + term
    return _silu(y)


def _gdn_kernel(blocks_per_seq, n_blocks, q_ref, k_ref, v_ref, cw_ref, gb_ref, ng_ref, o_ref,
                qbuf, kbuf, vbuf, state_ref, low_s, t_s, kp_s, qg_s, rhs_s, gl_s):
    rows = q_ref.shape[1]
    n = GDN_CHUNK
    dv = GDN_V_HEAD_DIM
    n_sys = 2 * (rows // n)
    g_step = pl.program_id(0)
    slot = g_step % 2

    @pl.when(g_step == 0)
    def _():
        for ref in (low_s, t_s, kp_s, qg_s, rhs_s, gl_s, state_ref):
            ref[...] = jnp.zeros_like(ref)

    @pl.when(g_step % blocks_per_seq == 0)
    def _():
        for ref in (qbuf, kbuf, vbuf):
            ref[0:CONV_HALO, :] = jnp.zeros((CONV_HALO, ref.shape[1]), _F32)

    ii = lax.broadcasted_iota(jnp.int32, (n, n), 0)
    jj = lax.broadcasted_iota(jnp.int32, (n, n), 1)
    causal = ii >= jj
    strict = ii > jj

    def recurrence_stage():
        fresh = (g_step - 2) % blocks_per_seq == 0
        ng = ng_ref[...]
        prods = []
        for idx in range(n_sys):
            wu = _dot(t_s[idx], rhs_s[slot, idx]).astype(_BF16)
            both = _dot(kp_s[slot, idx], wu)
            q_eff = (qg_s[slot, idx] - both[n:, :n]).astype(_BF16)
            prods.append((jnp.concatenate([q_eff, both[:n, :n].astype(_BF16)], 0), both[:n, n:],
                          both[n:, n:], gl_s[slot, idx, 0:1, :]))
            yield
        states = [jnp.where(fresh, 0.0, state_ref[e]) for e in range(2)]
        for idx, (qa, n_mat, pu, gl) in enumerate(prods):
            c, e = divmod(idx, 2)
            qa_s = _dot(qa, states[e].astype(_BF16))
            out = qa_s[:n] + pu
            states[e] = states[e] * gl - qa_s[n:] + n_mat
            out = out * lax.rsqrt(jnp.mean(out * out, -1, keepdims=True) + NORM_EPS) * ng
            o_ref[0, c * n:(c + 1) * n, e * dv:(e + 1) * dv] = out.astype(o_ref.dtype)
            yield
        state_ref[0] = states[0]
        state_ref[1] = states[1]

    def inverse_stage():
        for t_invs in _unit_lower_inverses([low_s[idx] for idx in range(n_sys)], ii, jj):
            yield
        for idx, t_inv in enumerate(t_invs):
            t_s[idx] = t_inv

    def operand_stage():
        bufs = ((qbuf, q_ref), (kbuf, k_ref), (vbuf, v_ref))
        for buf, ref in bufs:
            buf[CONV_HALO:CONV_HALO + rows, :] = ref[0].astype(_F32)
        yield
        head = (jnp.minimum(g_step, n_blocks - 1) // blocks_per_seq) % N_QK_HEADS
        wq = cw_ref[head]
        wk = cw_ref[N_QK_HEADS + head]
        wv = jnp.concatenate([cw_ref[2 * N_QK_HEADS + 2 * head], cw_ref[2 * N_QK_HEADS + 2 * head + 1]], 1)
        for c in range(rows // n):
            r0 = c * n
            qc = _conv_silu_rows(qbuf, wq, r0, n)
            kc = _conv_silu_rows(kbuf, wk, r0, n)
            qn = qc * (lax.rsqrt(jnp.sum(qc * qc, -1, keepdims=True) + NORM_EPS) * (QK_HEAD_DIM ** -0.5))
            kn = kc * lax.rsqrt(jnp.sum(kc * kc, -1, keepdims=True) + NORM_EPS)
            kb = kn.astype(_BF16)
            kq = _dot_nt(jnp.concatenate([kb, qn.astype(_BF16)], 0), kb)
            kk, qk = kq[:n], kq[n:]
            yield
            vc = _conv_silu_rows(vbuf, wv, r0, n)
            for e in range(2):
                idx = 2 * c + e
                g_row = jnp.broadcast_to(gb_ref[0, 0, 0, e:e + 1, r0:r0 + n], (n, n))
                g_col = g_row.T
                b_col = jnp.broadcast_to(gb_ref[0, 1, 0, e:e + 1, r0:r0 + n], (n, n)).T
                decay = jnp.where(causal, jnp.exp(g_col - g_row), 0.0)
                e_g = jnp.exp(g_col)
                g_last = g_col[n - 1:n, :]
                v_e = vc[:, e * dv:(e + 1) * dv]
                low_s[idx] = jnp.where(strict, b_col * kk * decay, 0.0)
                k_tail_t = (kn * jnp.exp(g_last - g_col)).T
                kp_s[slot, idx] = jnp.concatenate([k_tail_t, qk * decay], 0).astype(_BF16)
                qg_s[slot, idx] = qn * e_g
                rhs_s[slot, idx] = jnp.concatenate([b_col * e_g * kn, b_col * v_e], 1).astype(_BF16)
                gl_s[slot, idx] = jnp.broadcast_to(jnp.exp(g_last), (CONV_HALO, dv))
                yield
        for buf, _ in bufs:
            buf[0:CONV_HALO, :] = buf[rows:rows + CONV_HALO, :]

    n_chunks = rows // n
    _interleave((inverse_stage(), 12), (operand_stage(), 3 * n_chunks + 1), (recurrence_stage(), 2 * n_sys + 1))


def _gdn(h, conv_w, gates, norm_g, rows=GDN_ROWS):
    b, s, _ = h.shape
    rows = min(rows, s)
    nb = s // rows
    total = b * N_QK_HEADS * nb
    v_blk0 = 2 * QK_WIDTH // (2 * GDN_V_HEAD_DIM)
    gates = gates.reshape(b, 2, N_QK_HEADS, 2, s)
    conv_groups = conv_w.reshape(CONV_WIDTH, CONV_CH // LANES, LANES).transpose(1, 0, 2)
    pair = 2 * GDN_V_HEAD_DIM
    n_sys = 2 * (rows // GDN_CHUNK)

    def where(g):
        g = jnp.minimum(g, total - 1)
        return g // (N_QK_HEADS * nb), (g // nb) % N_QK_HEADS, g % nb

    def cur(col0):
        def index(g):
            i, hd, t = where(g)
            return i, t, col0 + hd
        return index

    def gate_index(g):
        i, hd, t = where(g)
        return i, 0, hd, 0, t

    def out_index(g):
        i, hd, t = where(jnp.maximum(g - 2, 0))
        return i, t, hd

    return pl.pallas_call(
        functools.partial(_gdn_kernel, nb, total),
        grid=(total + 2,),
        in_specs=[
            pl.BlockSpec((1, rows, QK_HEAD_DIM), cur(0)),
            pl.BlockSpec((1, rows, QK_HEAD_DIM), cur(N_QK_HEADS)),
            pl.BlockSpec((1, rows, pair), cur(v_blk0)),
            pl.BlockSpec((CONV_CH // LANES, CONV_WIDTH, LANES), lambda g: (0, 0, 0)),
            pl.BlockSpec((1, 2, 1, 2, rows), gate_index),
            pl.BlockSpec((1, GDN_V_HEAD_DIM), lambda g: (0, 0)),
        ],
        out_specs=pl.BlockSpec((1, rows, pair), out_index),
        out_shape=jax.ShapeDtypeStruct((b, s, TOK_WIDTH), _BF16),
        scratch_shapes=[pltpu.VMEM((rows + CONV_HALO, QK_HEAD_DIM), _F32),
                        pltpu.VMEM((rows + CONV_HALO, QK_HEAD_DIM), _F32),
                        pltpu.VMEM((rows + CONV_HALO, pair), _F32),
                        pltpu.VMEM((2, QK_HEAD_DIM, GDN_V_HEAD_DIM), _F32),
                        pltpu.VMEM((n_sys, GDN_CHUNK, GDN_CHUNK), _F32),
                        pltpu.VMEM((n_sys, GDN_CHUNK, GDN_CHUNK), _BF16),
                        pltpu.VMEM((2, n_sys, QK_HEAD_DIM + GDN_CHUNK, GDN_CHUNK), _BF16),
                        pltpu.VMEM((2, n_sys, GDN_CHUNK, QK_HEAD_DIM), _F32),
                        pltpu.VMEM((2, n_sys, GDN_CHUNK, QK_HEAD_DIM + GDN_V_HEAD_DIM), _BF16),
                        pltpu.VMEM((2, n_sys, CONV_HALO, GDN_V_HEAD_DIM), _F32)],
        compiler_params=_cparams(("arbitrary",)),
        name="gated_delta",
    )(h, h, h, conv_groups, gates, norm_g.reshape(1, GDN_V_HEAD_DIM))


def _out_kernel(tok_ref, mq_ref, z0_ref, z1_ref, z2_ref, z3_ref, kv_ref, w_ref, x_ref, g_ref, b_ref,
                o_ref, ob_ref, y_s):
    tm, d = x_ref.shape
    kc = MIX_WIDTH // OUT_K_SPLIT
    z_refs = (z0_ref, z1_ref, z2_ref, z3_ref)

    @pl.when(pl.program_id(0) == 0)
    def _():
        y_s[...] = jnp.zeros_like(y_s)

    def attend(m):
        lo = m * MEM_HEAD_DIM
        mk = kv_ref[0, :, lo:lo + MEM_HEAD_DIM]
        mv = kv_ref[0, :, MEM_WIDTH + lo:MEM_WIDTH + lo + MEM_HEAD_DIM]
        sc = _dot_nt(mq_ref[:, lo:lo + MEM_HEAD_DIM], mk) * (MEM_HEAD_DIM ** -0.5)
        p = jnp.exp(sc - jnp.max(sc, -1, keepdims=True))
        p = p / jnp.sum(p, -1, keepdims=True)
        return _dot(p.astype(_BF16), mv)

    def branch_chunk(k):
        lo = k * kc
        z = z_refs[lo // PROJ_TILE][:, lo % PROJ_TILE:lo % PROJ_TILE + kc]
        if lo < TOK_WIDTH:
            val = tok_ref[:, lo:lo + kc]
        else:
            first = (lo - TOK_WIDTH) // MEM_HEAD_DIM
            val = jnp.concatenate([attend(first + j) for j in range(kc // MEM_HEAD_DIM)], 1).astype(_BF16)
        return val * _silu(z)

    def project_stage():
        acc = None
        for k in range(OUT_K_SPLIT):
            part = _dot(branch_chunk(k), w_ref[0, k * kc:(k + 1) * kc, :])
            acc = part if acc is None else acc + part
            yield
        y_s[...] = acc

    def norm_stage():
        rows = tm // OUT_K_SPLIT
        for blk in range(OUT_K_SPLIT):
            sl = slice(blk * rows, (blk + 1) * rows)
            r = DEEPNORM_ALPHA * x_ref[sl, :] + y_s[sl, :]
            mu = jnp.mean(r, -1, keepdims=True)
            dev = r - mu
            var = jnp.mean(dev * dev, -1, keepdims=True)
            out = dev * lax.rsqrt(var + LN_EPS) * g_ref[...] + b_ref[...]
            o_ref[sl, :] = out
            ob_ref[sl, :] = out.astype(_BF16)
            yield

    _interleave((norm_stage(), OUT_K_SPLIT), (project_stage(), OUT_K_SPLIT + 1))


def _gate_out_ln(tok, h, kv, w_out_b, layer, x, ln_g, ln_b, seq, tm=OUT_ROWS):
    m, d = x.shape
    tm = min(tm, seq)
    n_tiles = m // tm
    n_mem = kv.shape[1]
    z_blk0 = (CONV_CH + MEM_WIDTH) // PROJ_TILE
    once = pl.Buffered(1)

    def gated(i):
        return jnp.minimum(i, n_tiles - 1)

    def normed(i):
        return jnp.maximum(i - 1, 0)

    def h_cols(blk):
        return pl.BlockSpec((tm, PROJ_TILE), lambda i: (gated(i), blk))

    return pl.pallas_call(
        _out_kernel,
        grid=(n_tiles + 1,),
        in_specs=[pl.BlockSpec((tm, TOK_WIDTH), lambda i: (gated(i), 0)),
                  h_cols(CONV_CH // PROJ_TILE),
                  h_cols(z_blk0), h_cols(z_blk0 + 1), h_cols(z_blk0 + 2), h_cols(z_blk0 + 3),
                  pl.BlockSpec((1, n_mem, 2 * MEM_WIDTH), lambda i: (gated(i) * tm // seq, 0, layer)),
                  pl.BlockSpec((1, MIX_WIDTH, d), lambda i: (layer, 0, 0), pipeline_mode=once),
                  pl.BlockSpec((tm, d), lambda i: (normed(i), 0)),
                  pl.BlockSpec((1, d), lambda i: (0, 0)),
                  pl.BlockSpec((1, d), lambda i: (0, 0))],
        out_specs=[pl.BlockSpec((tm, d), lambda i: (normed(i), 0))] * 2,
        out_shape=[jax.ShapeDtypeStruct((m, d), _F32), jax.ShapeDtypeStruct((m, d), _BF16)],
        scratch_shapes=[pltpu.VMEM((tm, d), _F32)],
        compiler_params=_cparams(("arbitrary",)),
        name="gate_out_layernorm",
    )(tok, h, h, h, h, h, kv, w_out_b, x, ln_g.reshape(1, d), ln_b.reshape(1, d))


def kernel(x, mem, positions, w_in_ret, ret_norm_g, w_in_gdn, conv_w, a_log, dt_bias, gdn_norm_g,
           w_mem_kv, w_out, ln_g, ln_b):
    b, s, d = x.shape
    m = b * s
    n_mem = mem.shape[1]
    cos, sin = _rope_tables(positions)
    mem_b = mem.reshape(b * n_mem, d).astype(_BF16)
    w_out_b = w_out.astype(_BF16)
    w_in_gdn_t = jnp.swapaxes(w_in_gdn, 1, 2)
    kv = _project(mem_b, w_mem_kv, None, 2 * MEM_WIDTH).reshape(b, n_mem, DEPTH * 2 * MEM_WIDTH)
    xf = x.reshape(m, d)
    xb = xf.astype(_BF16)
    for i in range(DEPTH):
        j = i // 2
        if i % 2 == 0:
            h = _project(xb, w_in_ret, j, RET_COLS)
            tok = _retention(h.reshape(b, s, RET_COLS), cos, sin, ret_norm_g[j])
        else:
            h = _project(xb, w_in_gdn_t, j, RET_COLS, w_is_transposed=True)
            gates = _gdn_gates(xb.reshape(b, s, d), w_in_gdn[j, :, RET_COLS:], a_log[j], dt_bias[j])
            tok = _gdn(h.reshape(b, s, RET_COLS), conv_w[j], gates, gdn_norm_g[j])
        xf, xb = _gate_out_ln(tok.reshape(m, TOK_WIDTH), h, kv, w_out_b, i, xf, ln_g[i], ln_b[i], s)
    return xf.reshape(b, s, d)
```
